```python
import jax
import jax.numpy as jnp
from jax import lax
import numpy as np

D_MODEL = 1024
BATCH = 4
SEQ = 4096
DEPTH = 4

GRID_W = 64
CTX_LEN = 256
N_MIXERS = 3
CHUNK = 128
A_DIM = 2 * D_MODEL
A_GROUPS = 8
N_HEADS = 16
N_KV = 4
HEAD_DIM = D_MODEL // N_HEADS
ROPE_FREQS = HEAD_DIM // 4
ROPE_THETA = 10000.0
Q_BLOCK = 128
POOL_WINDOWS = (2, 4, 8, 16)
POOL_GROUP = D_MODEL // 4
FFN_DIM = 2816
N_EXPERTS = 8
TOP_K = 2
EXPERT_DIM = 3584
EPS = 1e-6

N_A = (DEPTH + 2) // 3
N_B = (DEPTH + 1) // 3
N_C = DEPTH // 3
N_DENSE = (DEPTH + 1) // 2
N_MOE = DEPTH // 2

kernel_name = 'hybrid_gmlp_gqa_pool_moe_diffusion_trunk'


def rms_norm(x, g):
    xf = x.astype(jnp.float32)
    y = xf * lax.rsqrt(jnp.mean(xf * xf, axis=-1, keepdims=True) + EPS)
    return (y * g.astype(jnp.float32)).astype(x.dtype)


def adaln(cond, w, b):
    m = jax.nn.silu(cond) @ w + b
    m = m.reshape(m.shape[:-1] + (1, m.shape[-1]))
    if m.ndim == 2:
        m = m[None]
    return jnp.split(m, 6, axis=-1)


def modulate(x, g, shift, scale):
    return rms_norm(x, g) * (1 + scale) + shift


def swiglu(t, wg, wu, wd):
    return (jax.nn.silu(t @ wg) * (t @ wu)) @ wd


def moe_swiglu(h, router, wg, wu, wd):
    b, l, d = h.shape
    t = h.reshape(b * l, d)
    logits = (t @ router).astype(jnp.float32)
    top_v, top_i = lax.top_k(logits, TOP_K)
    gates = jax.nn.softmax(top_v, axis=-1)
    combine = jnp.sum(jax.nn.one_hot(top_i, N_EXPERTS, dtype=jnp.float32) * gates[..., None], axis=1)
    combine = combine.astype(t.dtype)
    out = jnp.zeros_like(t)
    for e in range(N_EXPERTS):
        out = out + combine[:, e:e + 1] * swiglu(t, wg[e], wu[e], wd[e])
    return out.reshape(b, l, d)


def chunk_mlp(h, w_in, v_g, w_s, b_s, w_out):
    b, l, _ = h.shape
    z = jax.nn.gelu(h @ w_in)
    u, v = jnp.split(z, 2, axis=-1)
    v = rms_norm(v, v_g).reshape(b, l // CHUNK, CHUNK, A_GROUPS, A_DIM // A_GROUPS)
    mixed = jnp.einsum('gts,bcsgd->bctgd', w_s, v) + b_s.T[:, :, None]
    return (u * mixed.reshape(b, l, A_DIM)) @ w_out


def centred_pool_minus_identity(x, w):
    b, l, ch = x.shape
    xf = x.astype(jnp.float32)
    cs = jnp.concatenate([jnp.zeros((b, 1, ch), jnp.float32), jnp.cumsum(xf, axis=1)], axis=1)
    t = jnp.arange(l)
    lo = jnp.clip(t - w // 2, 0, l - 1)
    hi = jnp.clip(t + w // 2 - 1, 0, l - 1)
    cnt = (hi - lo + 1).astype(jnp.float32)
    s = jnp.take(cs, hi + 1, axis=1) - jnp.take(cs, lo, axis=1)
    return (s / cnt[None, :, None] - xf).astype(x.dtype)


def pool_mixer(h, p_w, p_scale):
    groups = jnp.split(h, len(POOL_WINDOWS), axis=-1)
    outs = [centred_pool_minus_identity(grp, w) @ p_w[j]
            for j, (grp, w) in enumerate(zip(groups, POOL_WINDOWS))]
    return jnp.concatenate(outs, axis=-1) * p_scale


def axial_rope_tables(rows):
    r = jnp.repeat(jnp.arange(rows), GRID_W)
    col = jnp.tile(jnp.arange(GRID_W), rows)
    inv = ROPE_THETA ** (-jnp.arange(ROPE_FREQS, dtype=jnp.float32) / ROPE_FREQS)
    ang = jnp.stack([r, col], axis=-1).astype(jnp.float32)[..., None] * inv
    return jnp.cos(ang), jnp.sin(ang)


def apply_rope(x, cos, sin):
    shp = x.shape
    xr = x.astype(jnp.float32).reshape(shp[:-1] + (2, 2, ROPE_FREQS))
    x1, x2 = xr[..., 0, :], xr[..., 1, :]
    bshape = (1, shp[1]) + (1,) * (len(shp) - 3) + cos.shape[1:]
    c, s = cos.reshape(bshape), sin.reshape(bshape)
    out = jnp.stack([x1 * c - x2 * s, x2 * c + x1 * s], axis=-2)
    return out.reshape(shp).astype(x.dtype)


def blocked_attention(q, k, v):
    b, lq = q.shape[:2]
    nb = lq // Q_BLOCK
    qb = jnp.moveaxis(q.reshape((b, nb, Q_BLOCK) + q.shape[2:]), 1, 0)
    scale = HEAD_DIM ** -0.5

    def one_block(qi):
        s = jnp.einsum('bqgrd,bkgd->bgrqk', qi, k, preferred_element_type=jnp.float32) * scale
        p = jax.nn.softmax(s, axis=-1).astype(v.dtype)
        return jnp.einsum('bgrqk,bkgd->bqgrd', p, v)

    o = lax.map(one_block, qb)
    return jnp.moveaxis(o, 0, 1).reshape(b, lq, N_HEADS * HEAD_DIM)


def attention_mixer(hl, hc, w_qkv, q_g, k_g, w_o, cos, sin, ctx_out):
    nq = N_HEADS * HEAD_DIM
    rep = N_HEADS // N_KV

    def q_heads(qf):
        return rms_norm(qf.reshape(qf.shape[0], qf.shape[1], N_KV, rep, HEAD_DIM), q_g)

    def kv_heads(kvf):
        kf, vf = jnp.split(kvf, 2, axis=-1)
        kk = rms_norm(kf.reshape(kf.shape[0], kf.shape[1], N_KV, HEAD_DIM), k_g)
        return kk, vf.reshape(vf.shape[0], vf.shape[1], N_KV, HEAD_DIM)

    qkv_l = hl @ w_qkv
    ql = apply_rope(q_heads(qkv_l[..., :nq]), cos, sin)
    kl, vl = kv_heads(qkv_l[..., nq:])
    kl = apply_rope(kl, cos, sin)
    kc, vc = kv_heads(hc @ w_qkv[:, nq:])
    yl = blocked_attention(ql, jnp.concatenate([kc, kl], axis=1), jnp.concatenate([vc, vl], axis=1)) @ w_o
    yc = blocked_attention(q_heads(hc @ w_qkv[:, :nq]), kc, vc) @ w_o if ctx_out else None
    return yl, yc


def setup_inputs(seed: int = 0) -> dict:
    key = jax.random.key(seed)
    ks = jax.random.split(key, 32)
    nrm = jax.random.normal
    f32 = jnp.float32
    qkv_w = (N_HEADS + 2 * N_KV) * HEAD_DIM
    return {
        'x': nrm(ks[0], (BATCH, SEQ, D_MODEL), f32),
        'c': nrm(ks[1], (BATCH, D_MODEL), f32),
        'ctx': nrm(ks[2], (BATCH, CTX_LEN, D_MODEL), f32),
        'c_ctx': nrm(ks[3], (D_MODEL,), f32),
        'ada_w': nrm(ks[4], (DEPTH, D_MODEL, 6 * D_MODEL), f32) * (0.5 * D_MODEL ** -0.5),
        'ada_b': nrm(ks[5], (DEPTH, 6 * D_MODEL), f32) * 0.01,
        'norm_g': 1.0 + 0.02 * nrm(ks[6], (DEPTH, 2, D_MODEL), f32),
        'a_w_in': nrm(ks[7], (N_A, D_MODEL, 2 * A_DIM), f32) * D_MODEL ** -0.5,
        'a_v_g': 1.0 + 0.02 * nrm(ks[8], (N_A, A_DIM), f32),
        'a_ws': nrm(ks[9], (N_A, A_GROUPS, CHUNK, CHUNK), f32) * CHUNK ** -0.5,
        'a_bs': 1.0 + 0.02 * nrm(ks[10], (N_A, A_GROUPS, CHUNK), f32),
        'a_w_out': nrm(ks[11], (N_A, A_DIM, D_MODEL), f32) * A_DIM ** -0.5,
        'b_w_qkv': nrm(ks[12], (N_B, D_MODEL, qkv_w), f32) * D_MODEL ** -0.5,
        'b_q_g': 1.0 + 0.02 * nrm(ks[13], (N_B, HEAD_DIM), f32),
        'b_k_g': 1.0 + 0.02 * nrm(ks[14], (N_B, HEAD_DIM), f32),
        'b_w_o': nrm(ks[15], (N_B, N_HEADS * HEAD_DIM, D_MODEL), f32) * (N_HEADS * HEAD_DIM) ** -0.5,
        'p_w': nrm(ks[16], (N_C, len(POOL_WINDOWS), POOL_GROUP, POOL_GROUP), f32) * POOL_GROUP ** -0.5,
        'p_scale': 1.0 + 0.02 * nrm(ks[17], (N_C, D_MODEL), f32),
        'f_w_gate': nrm(ks[18], (N_DENSE, D_MODEL, FFN_DIM), f32) * D_MODEL ** -0.5,
        'f_w_up': nrm(ks[19], (N_DENSE, D_MODEL, FFN_DIM), f32) * D_MODEL ** -0.5,
        'f_w_down': nrm(ks[20], (N_DENSE, FFN_DIM, D_MODEL), f32) * FFN_DIM ** -0.5,
        'm_router': nrm(ks[21], (N_MOE, D_MODEL, N_EXPERTS), f32) * D_MODEL ** -0.5,
        'm_w_gate': nrm(ks[22], (N_MOE, N_EXPERTS, D_MODEL, EXPERT_DIM), f32) * D_MODEL ** -0.5,
        'm_w_up': nrm(ks[23], (N_MOE, N_EXPERTS, D_MODEL, EXPERT_DIM), f32) * D_MODEL ** -0.5,
        'm_w_down': nrm(ks[24], (N_MOE, N_EXPERTS, EXPERT_DIM, D_MODEL), f32) * EXPERT_DIM ** -0.5,
    }


def reference(x, c, ctx, c_ctx, ada_w, ada_b, norm_g, a_w_in, a_v_g, a_ws, a_bs, a_w_out,
              b_w_qkv, b_q_g, b_k_g, b_w_o, p_w, p_scale, f_w_gate, f_w_up, f_w_down,
              m_router, m_w_gate, m_w_up, m_w_down):
    ROWS = x.shape[1] // GRID_W
    cos, sin = axial_rope_tables(ROWS)
    ctx_layers = [i for i in range(DEPTH) if i % N_MIXERS == 1]
    last_ctx_read = max(ctx_layers) if ctx_layers else -1
    xc = ctx
    for i in range(DEPTH):
        kind = i % N_MIXERS
        j = i // N_MIXERS
        adv = i < last_ctx_read
        need_ctx = adv or i == last_ctx_read
        sl, scl, gl, sfl, scfl, gfl = adaln(c, ada_w[i], ada_b[i])
        hl = modulate(x, norm_g[i, 0], sl, scl)
        if need_ctx:
            sc, scc, gc, sfc, scfc, gfc = adaln(c_ctx, ada_w[i], ada_b[i])
            hc = modulate(xc, norm_g[i, 0], sc, scc)
        if kind == 0:
            yl = chunk_mlp(hl, a_w_in[j], a_v_g[j], a_ws[j], a_bs[j], a_w_out[j])
            yc = chunk_mlp(hc, a_w_in[j], a_v_g[j], a_ws[j], a_bs[j], a_w_out[j]) if adv else None
        elif kind == 1:
            yl, yc = attention_mixer(hl, hc, b_w_qkv[j], b_q_g[j], b_k_g[j], b_w_o[j], cos, sin, adv)
        else:
            yl = pool_mixer(hl, p_w[j], p_scale[j])
            yc = pool_mixer(hc, p_w[j], p_scale[j]) if adv else None
        x = x + gl * yl
        f = i // 2
        hl2 = modulate(x, norm_g[i, 1], sfl, scfl)
        if i % 2 == 0:
            x = x + gfl * swiglu(hl2, f_w_gate[f], f_w_up[f], f_w_down[f])
        else:
            x = x + gfl * moe_swiglu(hl2, m_router[f], m_w_gate[f], m_w_up[f], m_w_down[f])
        if adv:
            xc = xc + gc * yc
            hc2 = modulate(xc, norm_g[i, 1], sfc, scfc)
            if i % 2 == 0:
                xc = xc + gfc * swiglu(hc2, f_w_gate[f], f_w_up[f], f_w_down[f])
            else:
                xc = xc + gfc * moe_swiglu(hc2, m_router[f], m_w_gate[f], m_w_up[f], m_w_down[f])
    return x
```

```python
import functools

import jax
import jax.numpy as jnp
import numpy as np
from jax import lax
from jax.experimental import pallas as pl
from jax.experimental.pallas import tpu as pltpu

D = 1024
BATCH = 4
SEQ = 4096
DEPTH = 4
GRID_W = 64
CTX_LEN = 256
CHUNK = 128
A_DIM = 2 * D
A_GROUPS = 8
A_GW = A_DIM // A_GROUPS
N_HEADS = 16
N_KV = 4
HEAD_DIM = 64
ROPE_FREQS = HEAD_DIM // 4
ROPE_THETA = 10000.0
POOL_WINDOWS = (2, 4, 8, 16)
POOL_GROUP = D // 4
FFN_DIM = 2816
N_EXPERTS = 8
EXPERT_DIM = 3584
EPS = 1e-6

N_LAT = BATCH * SEQ
N_CTX = BATCH * CTX_LEN
N_ALL = N_LAT + N_CTX
LK = CTX_LEN + SEQ
MOD_ROWS = 8
LANES = 128
QK_COLS = (N_HEADS + N_KV) * HEAD_DIM
QKV_COLS = (N_HEADS + 2 * N_KV) * HEAD_DIM

VMEM_LIMIT = 56 * 1024 * 1024

bf16 = jnp.bfloat16
f32 = jnp.float32


def _params(sem, vmem=VMEM_LIMIT):
    return pltpu.CompilerParams(dimension_semantics=sem, vmem_limit_bytes=vmem)


def _mod_row(t, tm):
    start = t * tm
    return jnp.where(start < N_LAT, start // SEQ, BATCH)


def _mod_spec(chunk, tm):
    return pl.BlockSpec((None, None, 1, D), lambda t, *_: (chunk, _mod_row(t, tm), 0, 0))


def _full_spec(shape):
    nd = len(shape)
    return pl.BlockSpec(shape, lambda *_: (0,) * nd)


def _modulate(x, g, shift, scale):
    ms = jnp.mean(x * x, axis=-1, keepdims=True)
    return (x * lax.rsqrt(ms + EPS) * g) * (1.0 + scale) + shift


def _gelu_tanh(z):
    return 0.5 * z * (1.0 + jnp.tanh(0.7978845608028654 * (z + 0.044715 * (z * z * z))))


def _dot(a, b):
    return jnp.dot(a, b, preferred_element_type=f32)


ADA_TN = 1536


def _ada_kernel(cond_ref, w_ref, b_ref, o_ref):
    cond = cond_ref[...]
    s = (cond * jax.nn.sigmoid(cond)).astype(bf16)
    o_ref[...] = _dot(s, w_ref[...].astype(bf16)) + b_ref[...]


def _ada_all(cond, ada_w, ada_b):
    return pl.pallas_call(
        _ada_kernel,
        out_shape=jax.ShapeDtypeStruct((DEPTH, MOD_ROWS, 6 * D), f32),
        grid=(DEPTH, 6 * D // ADA_TN),
        in_specs=[
            _full_spec((MOD_ROWS, D)),
            pl.BlockSpec((None, D, ADA_TN), lambda i, n: (i, 0, n)),
            pl.BlockSpec((None, 1, ADA_TN), lambda i, n: (i, 0, n)),
        ],
        out_specs=pl.BlockSpec((None, MOD_ROWS, ADA_TN), lambda i, n: (i, 0, n)),
        compiler_params=_params(("parallel", "parallel")),
        name="ada",
    )(cond, ada_w, ada_b.reshape(DEPTH, 1, 6 * D))


GMLP_TM = 256


def _gmlp_kernel(x_ref, sh_ref, sc_ref, gt_ref, g_ref, win_ref, vg_ref, ws_ref, bs_ref, wout_ref,
                 o_ref, uv_ref):
    x = x_ref[...]
    h = _modulate(x, g_ref[...], sh_ref[...], sc_ref[...]).astype(bf16)
    z = _gelu_tanh(_dot(h, win_ref[...]))
    u = z[:, :A_DIM]
    v = z[:, A_DIM:]
    v = (v * lax.rsqrt(jnp.mean(v * v, axis=-1, keepdims=True) + EPS) * vg_ref[...]).astype(bf16)
    for c in range(GMLP_TM // CHUNK):
        r0 = c * CHUNK
        for g in range(A_GROUPS):
            c0 = g * A_GW
            mixed = _dot(ws_ref[g], v[r0:r0 + CHUNK, c0:c0 + A_GW]) + bs_ref[:, c0:c0 + A_GW]
            uv_ref[r0:r0 + CHUNK, c0:c0 + A_GW] = (u[r0:r0 + CHUNK, c0:c0 + A_GW] * mixed).astype(bf16)
    y = _dot(uv_ref[...], wout_ref[...])
    o_ref[...] = x + gt_ref[...] * y


def _gmlp_mixer(x, mods, g, w_in, v_g, w_s, b_full, w_out):
    rows = x.shape[0]
    tm = GMLP_TM
    return pl.pallas_call(
        _gmlp_kernel,
        out_shape=jax.ShapeDtypeStruct((rows, D), f32),
        grid=(rows // tm,),
        in_specs=[
            pl.BlockSpec((tm, D), lambda t: (t, 0)),
            _mod_spec(0, tm), _mod_spec(1, tm), _mod_spec(2, tm),
            _full_spec((1, D)),
            _full_spec((D, 2 * A_DIM)),
            _full_spec((1, A_DIM)),
            _full_spec((A_GROUPS, CHUNK, CHUNK)),
            _full_spec((CHUNK, A_DIM)),
            _full_spec((A_DIM, D)),
        ],
        out_specs=pl.BlockSpec((tm, D), lambda t: (t, 0)),
        scratch_shapes=[pltpu.VMEM((tm, A_DIM), bf16)],
        compiler_params=_params(("parallel",)),
        name="gmlp_mixer",
    )(x, mods, mods, mods, g, w_in, v_g, w_s, b_full, w_out)


FFN_TM = 512
FFN_TF = 1408


def _ffn_kernel(x_ref, sh_ref, sc_ref, gt_ref, g_ref, wg_ref, wu_ref, wd_ref, o_ref, h_ref, acc_ref):
    j = pl.program_id(1)

    @pl.when(j == 0)
    def _():
        h_ref[...] = _modulate(x_ref[...], g_ref[...], sh_ref[...], sc_ref[...]).astype(bf16)
        acc_ref[...] = jnp.zeros_like(acc_ref)

    h = h_ref[...]
    a = _dot(h, wg_ref[...])
    b = _dot(h, wu_ref[...])
    act = (a * jax.nn.sigmoid(a) * b).astype(bf16)
    acc_ref[...] += _dot(act, wd_ref[...])

    @pl.when(j == pl.num_programs(1) - 1)
    def _():
        o_ref[...] = x_ref[...] + gt_ref[...] * acc_ref[...]


def _dense_ffn(x, mods, g, wg, wu, wd):
    rows = x.shape[0]
    tm, tf = FFN_TM, FFN_TF
    return pl.pallas_call(
        _ffn_kernel,
        out_shape=jax.ShapeDtypeStruct((rows, D), f32),
        grid=(rows // tm, FFN_DIM // tf),
        in_specs=[
            pl.BlockSpec((tm, D), lambda t, j: (t, 0)),
            _mod_spec(3, tm), _mod_spec(4, tm), _mod_spec(5, tm),
            _full_spec((1, D)),
            pl.BlockSpec((D, tf), lambda t, j: (0, j)),
            pl.BlockSpec((D, tf), lambda t, j: (0, j)),
            pl.BlockSpec((tf, D), lambda t, j: (j, 0)),
        ],
        out_specs=pl.BlockSpec((tm, D), lambda t, j: (t, 0)),
        scratch_shapes=[pltpu.VMEM((tm, D), bf16), pltpu.VMEM((tm, D), f32)],
        compiler_params=_params(("parallel", "arbitrary")),
        name="dense_ffn",
    )(x, mods, mods, mods, g, wg, wu, wd)


QKV_TM = 512
HEADS_PER_VREG = LANES // HEAD_DIM


def _qkv_kernel(x_ref, sh_ref, sc_ref, g_ref, w_ref, gain_ref, bd_ref, cos_ref, sin_ref, o_ref):
    h = _modulate(x_ref[...], g_ref[...], sh_ref[...], sc_ref[...]).astype(bf16)
    qkv = _dot(h, w_ref[...])
    lane = lax.broadcasted_iota(jnp.int32, (QKV_TM, LANES), 1)
    first_half = (lane % HEAD_DIM) < (HEAD_DIM // 2)
    cos = cos_ref[...]
    sin = sin_ref[...]
    bd = bd_ref[...]
    for cb in range(QKV_COLS // LANES):
        c0 = cb * LANES
        blk = qkv[:, c0:c0 + LANES]
        if c0 < QK_COLS:
            sq = blk * blk
            hi = sq.astype(bf16)
            lo = (sq - hi.astype(f32)).astype(bf16)
            ms = (_dot(hi, bd) + _dot(lo, bd)) * (1.0 / HEAD_DIM)
            y = blk * lax.rsqrt(ms + EPS) * gain_ref[:, c0:c0 + LANES]
            partner = jnp.where(first_half, pltpu.roll(y, LANES - HEAD_DIM // 2, 1),
                                pltpu.roll(y, HEAD_DIM // 2, 1))
            y = y * cos + partner * sin
            if c0 < N_HEADS * HEAD_DIM:
                y = y * (HEAD_DIM ** -0.5)
            blk = y
        o_ref[:, c0:c0 + LANES] = blk.astype(bf16)


def _qkv_proj(x_all, mods, g, w_qkv, gain, bd, cos_t, sin_t):
    tm = QKV_TM
    tiles_per_seq = SEQ // tm

    def tab_idx(t):
        return (jnp.where(t * tm < N_LAT, t % tiles_per_seq, tiles_per_seq), 0)

    return pl.pallas_call(
        _qkv_kernel,
        out_shape=jax.ShapeDtypeStruct((N_ALL, QKV_COLS), bf16),
        grid=(N_ALL // tm,),
        in_specs=[
            pl.BlockSpec((tm, D), lambda t: (t, 0)),
            _mod_spec(0, tm), _mod_spec(1, tm),
            _full_spec((1, D)),
            _full_spec((D, QKV_COLS)),
            _full_spec((1, QK_COLS)),
            _full_spec((LANES, LANES)),
            pl.BlockSpec((tm, LANES), tab_idx),
            pl.BlockSpec((tm, LANES), tab_idx),
        ],
        out_specs=pl.BlockSpec((tm, QKV_COLS), lambda t: (t, 0)),
        compiler_params=_params(("parallel",)),
        name="qkv_proj",
    )(x_all, mods, mods, g, w_qkv, gain, bd, cos_t, sin_t)


ATT_TQ = 256
REP = N_HEADS // N_KV


def _attn_kernel(q_ref, kk_ref, vv_ref, o_ref):
    lane = lax.broadcasted_iota(jnp.int32, (ATT_TQ, LANES), 1)
    low = lane < HEAD_DIM
    kk = kk_ref[...]
    vv = vv_ref[...]
    zero = jnp.zeros((ATT_TQ, LANES), bf16)
    for p in range(REP // HEADS_PER_VREG):
        q2 = q_ref[:, p * LANES:(p + 1) * LANES]
        outs = []
        for half in range(HEADS_PER_VREG):
            qm = jnp.where(low if half == 0 else ~low, q2, zero)
            s = _dot(qm, kk)
            m = jnp.max(s, axis=-1, keepdims=True)
            e = jnp.exp(s - m)
            l = jnp.sum(e, axis=-1, keepdims=True)
            outs.append(_dot(e.astype(bf16), vv) / l)
        o_ref[:, p * LANES:(p + 1) * LANES] = jnp.where(low, outs[0], outs[1]).astype(bf16)


def _attention(qkv, kk, vv):
    tq = ATT_TQ
    qt_per_seq = SEQ // tq
    gw = REP * HEAD_DIM
    return pl.pallas_call(
        _attn_kernel,
        out_shape=jax.ShapeDtypeStruct((N_LAT, D), bf16),
        grid=(BATCH, N_KV, qt_per_seq),
        in_specs=[
            pl.BlockSpec((tq, gw), lambda b, g, t: (b * qt_per_seq + t, g)),
            pl.BlockSpec((None, None, LANES, LK), lambda b, g, t: (b, g, 0, 0)),
            pl.BlockSpec((None, None, LK, LANES), lambda b, g, t: (b, g, 0, 0)),
        ],
        out_specs=pl.BlockSpec((tq, gw), lambda b, g, t: (b * qt_per_seq + t, g)),
        compiler_params=_params(("parallel", "parallel", "arbitrary")),
        name="attention",
    )(qkv, kk, vv)


PROJ_TM = 512


def _oproj_kernel(o_ref, x_ref, gt_ref, w_ref, out_ref):
    out_ref[...] = x_ref[...] + gt_ref[...] * _dot(o_ref[...], w_ref[...])


def _attn_out(o, x_all, mods, w_o):
    tm = PROJ_TM
    return pl.pallas_call(
        _oproj_kernel,
        out_shape=jax.ShapeDtypeStruct((N_LAT, D), f32),
        grid=(N_LAT // tm,),
        in_specs=[
            pl.BlockSpec((tm, D), lambda t: (t, 0)),
            pl.BlockSpec((tm, D), lambda t: (t, 0)),
            _mod_spec(2, tm),
            _full_spec((D, D)),
        ],
        out_specs=pl.BlockSpec((tm, D), lambda t: (t, 0)),
        compiler_params=_params(("parallel",)),
        name="attn_out",
    )(o, x_all, mods, w_o)


POOL_TM = 512
POOL_HALO = 128
POOL_EXT = POOL_TM + 2 * POOL_HALO


def _pool_kernel(xp_ref, x_ref, xn_ref, sh_ref, sc_ref, gt_ref, g_ref, band_ref, pw_ref, ps_ref, o_ref):
    t = pl.program_id(0)
    tiles_per_seq = SEQ // POOL_TM
    ts = t % tiles_per_seq
    x = x_ref[...]
    g, sh, sc = g_ref[...], sh_ref[...], sc_ref[...]
    h = _modulate(x, g, sh, sc)
    hp = jnp.where(ts > 0, _modulate(xp_ref[...], g, sh, sc), 0.0)
    hn = jnp.where(ts < tiles_per_seq - 1, _modulate(xn_ref[...], g, sh, sc), 0.0)
    ext = jnp.concatenate([hp, h, hn], axis=0)
    e_hi = ext.astype(bf16)
    r1 = ext - e_hi.astype(f32)
    e_mid = r1.astype(bf16)
    e_lo = (r1 - e_mid.astype(f32)).astype(bf16)
    pos = ts * POOL_TM + lax.broadcasted_iota(jnp.int32, (POOL_TM, 1), 0)
    for j, w in enumerate(POOL_WINDOWS):
        c0 = j * POOL_GROUP
        band = band_ref[j]
        s = (_dot(band, e_hi[:, c0:c0 + POOL_GROUP]) + _dot(band, e_mid[:, c0:c0 + POOL_GROUP])
             + _dot(band, e_lo[:, c0:c0 + POOL_GROUP]))
        lo_i = jnp.maximum(pos - w // 2, 0)
        hi_i = jnp.minimum(pos + w // 2 - 1, SEQ - 1)
        cnt = (hi_i - lo_i + 1).astype(f32)
        d = (s / cnt - h[:, c0:c0 + POOL_GROUP]).astype(bf16)
        y = _dot(d, pw_ref[j]) * ps_ref[:, c0:c0 + POOL_GROUP]
        o_ref[:, c0:c0 + POOL_GROUP] = x[:, c0:c0 + POOL_GROUP] + gt_ref[:, c0:c0 + POOL_GROUP] * y


def _pool_bands():
    t = np.arange(POOL_TM)[:, None] + POOL_HALO
    s = np.arange(POOL_EXT)[None, :]
    bands = [((s >= t - w // 2) & (s <= t + w // 2 - 1)) for w in POOL_WINDOWS]
    return jnp.asarray(np.stack(bands).astype(np.float32), dtype=bf16)


def _pool_mixer(x, mods, g, p_w, p_scale):
    tm = POOL_TM
    hb = tm // POOL_HALO
    n_halo_blocks = N_LAT // POOL_HALO
    return pl.pallas_call(
        _pool_kernel,
        out_shape=jax.ShapeDtypeStruct((N_LAT, D), f32),
        grid=(N_LAT // tm,),
        in_specs=[
            pl.BlockSpec((POOL_HALO, D), lambda t: (jnp.maximum(t * hb - 1, 0), 0)),
            pl.BlockSpec((tm, D), lambda t: (t, 0)),
            pl.BlockSpec((POOL_HALO, D), lambda t: (jnp.minimum((t + 1) * hb, n_halo_blocks - 1), 0)),
            _mod_spec(0, tm), _mod_spec(1, tm), _mod_spec(2, tm),
            _full_spec((1, D)),
            _full_spec((len(POOL_WINDOWS), POOL_TM, POOL_EXT)),
            _full_spec((len(POOL_WINDOWS), POOL_GROUP, POOL_GROUP)),
            _full_spec((1, D)),
        ],
        out_specs=pl.BlockSpec((tm, D), lambda t: (t, 0)),
        compiler_params=_params(("parallel",)),
        name="pool_mixer",
    )(x, x, x, mods, mods, mods, g, _pool_bands(), p_w, p_scale)


ROUTER_TM = 512


def _router_kernel(x_ref, sh_ref, sc_ref, g_ref, wr_ref, h_ref, route_ref):
    h = _modulate(x_ref[...], g_ref[...], sh_ref[...], sc_ref[...])
    h_ref[...] = h.astype(bf16)
    logits = jnp.dot(h, wr_ref[...], preferred_element_type=f32, precision=lax.Precision.HIGHEST)
    lane = lax.broadcasted_iota(jnp.int32, logits.shape, 1)
    neg = jnp.float32(-jnp.inf)
    logits = jnp.where(lane < N_EXPERTS, logits, neg)
    m1 = jnp.max(logits, axis=-1, keepdims=True)
    i1 = jnp.min(jnp.where(logits == m1, lane, LANES), axis=-1, keepdims=True)
    rest = jnp.where(lane == i1, neg, logits)
    m2 = jnp.max(rest, axis=-1, keepdims=True)
    i2 = jnp.min(jnp.where(rest == m2, lane, LANES), axis=-1, keepdims=True)
    e2 = jnp.exp(m2 - m1)
    g1 = 1.0 / (1.0 + e2)
    g2 = e2 / (1.0 + e2)
    route = jnp.where(lane == 0, i1.astype(f32),
                      jnp.where(lane == 1, i2.astype(f32),
                                jnp.where(lane == 2, g1, jnp.where(lane == 3, g2, 0.0))))
    route_ref[...] = route


def _router(x, mods, g, wr_pad):
    tm = ROUTER_TM
    return pl.pallas_call(
        _router_kernel,
        out_shape=(jax.ShapeDtypeStruct((N_LAT, D), bf16), jax.ShapeDtypeStruct((N_LAT, LANES), f32)),
        grid=(N_LAT // tm,),
        in_specs=[
            pl.BlockSpec((tm, D), lambda t: (t, 0)),
            _mod_spec(3, tm), _mod_spec(4, tm),
            _full_spec((1, D)),
            _full_spec((D, LANES)),
        ],
        out_specs=(pl.BlockSpec((tm, D), lambda t: (t, 0)), pl.BlockSpec((tm, LANES), lambda t: (t, 0))),
        compiler_params=_params(("parallel",)),
        name="router",
    )(x, mods, mods, g, wr_pad)


MOE_TM = 512
MOE_TF = 896
MOE_TILES = (2 * N_LAT) // MOE_TM + N_EXPERTS
MOE_ROWS = MOE_TILES * MOE_TM


def _expert_kernel(te_ref, nt_ref, xs_ref, wg_ref, wu_ref, wd_ref, y_ref, acc_ref):
    t = pl.program_id(0)
    j = pl.program_id(1)
    last = pl.num_programs(1) - 1
    active = t < nt_ref[0]

    @pl.when(active)
    def _():
        @pl.when(j == 0)
        def _():
            acc_ref[...] = jnp.zeros_like(acc_ref)

        xs = xs_ref[...]
        a = _dot(xs, wg_ref[...])
        b = _dot(xs, wu_ref[...])
        act = (a * jax.nn.sigmoid(a) * b).astype(bf16)
        acc_ref[...] += _dot(act, wd_ref[...])

        @pl.when(j == last)
        def _():
            y_ref[...] = acc_ref[...]

    @pl.when(jnp.logical_not(active) & (j == last))
    def _():
        y_ref[...] = jnp.zeros_like(y_ref)


def _experts(tile_expert, num_tiles, xs, wg, wu, wd):
    tm, tf = MOE_TM, MOE_TF
    nj = EXPERT_DIM // tf

    def w_col(t, j, te, nt):
        jj = jnp.where(t < nt[0], j, nj - 1)
        return (te[t], 0, jj)

    def w_row(t, j, te, nt):
        jj = jnp.where(t < nt[0], j, nj - 1)
        return (te[t], jj, 0)

    return pl.pallas_call(
        _expert_kernel,
        out_shape=jax.ShapeDtypeStruct((MOE_ROWS, D), f32),
        grid_spec=pltpu.PrefetchScalarGridSpec(
            num_scalar_prefetch=2,
            grid=(MOE_TILES, nj),
            in_specs=[
                pl.BlockSpec((tm, D), lambda t, j, te, nt: (jnp.minimum(t, nt[0] - 1), 0)),
                pl.BlockSpec((None, D, tf), w_col),
                pl.BlockSpec((None, D, tf), w_col),
                pl.BlockSpec((None, tf, D), w_row),
            ],
            out_specs=pl.BlockSpec((tm, D), lambda t, j, te, nt: (t, 0)),
            scratch_shapes=[pltpu.VMEM((tm, D), f32)],
        ),
        compiler_params=_params(("arbitrary", "arbitrary")),
        name="moe_experts",
    )(tile_expert, num_tiles, xs, wg, wu, wd)


def _moe(x, mods, g, wr_pad, wg, wu, wd):
    tm = MOE_TM
    h, route = _router(x, mods, g, wr_pad)
    ids = route[:, :2].astype(jnp.int32)
    gates = route[:, 2:4]
    flat_e = ids.reshape(-1)
    n_assign = flat_e.shape[0]
    onehot = (flat_e[:, None] == jnp.arange(N_EXPERTS, dtype=jnp.int32)[None, :]).astype(jnp.int32)
    csum = jnp.cumsum(onehot, axis=0)
    rank = jnp.take_along_axis(csum, flat_e[:, None], axis=1)[:, 0] - 1
    counts = csum[-1]
    tiles_per_e = (counts + tm - 1) // tm
    tile_end = jnp.cumsum(tiles_per_e)
    tile_start = tile_end - tiles_per_e
    pos = tile_start[flat_e] * tm + rank
    row_token = jnp.zeros((MOE_ROWS,), jnp.int32).at[pos].set(jnp.arange(n_assign, dtype=jnp.int32) // 2)
    num_tiles = tile_end[-1]
    tile_ids = jnp.arange(MOE_TILES, dtype=jnp.int32)
    tile_expert = jnp.sum(tile_end[None, :] <= jnp.minimum(tile_ids, num_tiles - 1)[:, None], axis=1)
    tile_expert = jnp.minimum(tile_expert, N_EXPERTS - 1).astype(jnp.int32)
    xs = jnp.take(h, row_token, axis=0)
    y = _experts(tile_expert, num_tiles.reshape(1).astype(jnp.int32), xs, wg, wu, wd)
    yy = jnp.take(y, pos, axis=0).reshape(N_LAT, 2, D)
    mix = yy[:, 0] * gates[:, 0:1] + yy[:, 1] * gates[:, 1:2]
    gate_rows = jnp.repeat(mods[5, :BATCH, 0], SEQ, axis=0)
    return x + gate_rows * mix


def _rope_tables():
    rows = SEQ // GRID_W
    r = jnp.repeat(jnp.arange(rows), GRID_W)
    col = jnp.tile(jnp.arange(GRID_W), rows)
    inv = ROPE_THETA ** (-jnp.arange(ROPE_FREQS, dtype=f32) / ROPE_FREQS)
    ang = jnp.stack([r, col], axis=-1).astype(f32)[..., None] * inv
    c = jnp.cos(ang).reshape(SEQ, 2 * ROPE_FREQS)
    s = jnp.sin(ang).reshape(SEQ, 2 * ROPE_FREQS)
    cos_h = jnp.concatenate([c, c], axis=-1)
    sin_h = jnp.concatenate([-s, s], axis=-1)
    cos_t = jnp.tile(cos_h, (1, HEADS_PER_VREG))
    sin_t = jnp.tile(sin_h, (1, HEADS_PER_VREG))
    cos_t = jnp.concatenate([cos_t, jnp.ones((QKV_TM, LANES), f32)], axis=0)
    sin_t = jnp.concatenate([sin_t, jnp.zeros((QKV_TM, LANES), f32)], axis=0)
    return cos_t, sin_t


def kernel(x, c, ctx, c_ctx, ada_w, ada_b, norm_g, a_w_in, a_v_g, a_ws, a_bs, a_w_out, b_w_qkv, b_q_g, b_k_g,
           b_w_o, p_w, p_scale, f_w_gate, f_w_up, f_w_down, m_router, m_w_gate, m_w_up, m_w_down):
    cond = jnp.concatenate([c, c_ctx[None, :], jnp.zeros((MOD_ROWS - BATCH - 1, D), f32)], axis=0)
    ada = _ada_all(cond, ada_w, ada_b)
    mods = ada.reshape(DEPTH, MOD_ROWS, 6, 1, D).transpose(0, 2, 1, 3, 4)

    def gmlp_weights(j):
        b_full = jnp.repeat(a_bs[j].T, A_GW, axis=1)
        return (a_w_in[j].astype(bf16), a_v_g[j][None, :], a_ws[j].astype(bf16), b_full,
                a_w_out[j].astype(bf16))

    def moe_weights(f):
        wr_pad = jnp.pad(m_router[f], ((0, 0), (0, LANES - N_EXPERTS)))
        return wr_pad, m_w_gate[f].astype(bf16), m_w_up[f].astype(bf16), m_w_down[f].astype(bf16)

    x_all = jnp.concatenate([x.reshape(N_LAT, D), ctx.reshape(N_CTX, D)], axis=0)

    x_all = _gmlp_mixer(x_all, mods[0], norm_g[0, 0][None, :], *gmlp_weights(0))
    x_all = _dense_ffn(x_all, mods[0], norm_g[0, 1][None, :], f_w_gate[0].astype(bf16),
                       f_w_up[0].astype(bf16), f_w_down[0].astype(bf16))

    gain = jnp.concatenate([jnp.tile(b_q_g[0], N_HEADS), jnp.tile(b_k_g[0], N_KV)])[None, :]
    head_of_lane = np.arange(LANES) // HEAD_DIM
    bd = jnp.asarray((head_of_lane[:, None] == head_of_lane[None, :]).astype(np.float32), dtype=bf16)
    cos_t, sin_t = _rope_tables()
    qkv = _qkv_proj(x_all, mods[1], norm_g[1, 0][None, :], b_w_qkv[0].astype(bf16), gain, bd, cos_t, sin_t)

    def keys_first(a):
        return jnp.concatenate([a[N_LAT:].reshape(BATCH, CTX_LEN, -1), a[:N_LAT].reshape(BATCH, SEQ, -1)], axis=1)

    k = keys_first(qkv[:, N_HEADS * HEAD_DIM:QK_COLS]).reshape(BATCH, LK, N_KV, HEAD_DIM)
    v = keys_first(qkv[:, QK_COLS:]).reshape(BATCH, LK, N_KV, HEAD_DIM)
    kt = k.transpose(0, 2, 3, 1)
    kk = jnp.concatenate([kt] * HEADS_PER_VREG, axis=2)
    vt = v.transpose(0, 2, 1, 3)
    vv = jnp.concatenate([vt] * HEADS_PER_VREG, axis=3)
    o = _attention(qkv, kk, vv)
    xl = _attn_out(o, x_all, mods[1], b_w_o[0].astype(bf16))
    xl = _moe(xl, mods[1], norm_g[1, 1][None, :], *moe_weights(0))

    xl = _pool_mixer(xl, mods[2], norm_g[2, 0][None, :], p_w[0].astype(bf16), p_scale[0][None, :])
    xl = _dense_ffn(xl, mods[2], norm_g[2, 1][None, :], f_w_gate[1].astype(bf16),
                    f_w_up[1].astype(bf16), f_w_down[1].astype(bf16))

    xl = _gmlp_mixer(xl, mods[3], norm_g[3, 0][None, :], *gmlp_weights(1))
    xl = _moe(xl, mods[3], norm_g[3, 1][None, :], *moe_weights(1))
    return xl.reshape(BATCH, SEQ, D)
```

```python
import functools

import jax
import jax.numpy as jnp
import numpy as np
from jax import lax
from jax.experimental import pallas as pl
from jax.experimental.pallas import tpu as pltpu

D = 1024
BATCH = 4
SEQ = 4096
DEPTH = 4
GRID_W = 64
CTX_LEN = 256
CHUNK = 128
A_DIM = 2 * D
A_GROUPS = 8
A_GW = A_DIM // A_GROUPS
N_HEADS = 16
N_KV = 4
HEAD_DIM = 64
ROPE_FREQS = HEAD_DIM // 4
ROPE_THETA = 10000.0
POOL_WINDOWS = (2, 4, 8, 16)
POOL_GROUP = D // 4
FFN_DIM = 2816
N_EXPERTS = 8
EXPERT_DIM = 3584
EPS = 1e-6

N_LAT = BATCH * SEQ
N_CTX = BATCH * CTX_LEN
N_ALL = N_LAT + N_CTX
LK = CTX_LEN + SEQ
MOD_ROWS = 8
LANES = 128
QK_COLS = (N_HEADS + N_KV) * HEAD_DIM
QKV_COLS = (N_HEADS + 2 * N_KV) * HEAD_DIM

VMEM_LIMIT = 56 * 1024 * 1024

bf16 = jnp.bfloat16
f32 = jnp.float32


def _params(sem, vmem=VMEM_LIMIT):
    return pltpu.CompilerParams(dimension_semantics=sem, vmem_limit_bytes=vmem)


def _mod_row(t, tm):
    start = t * tm
    return jnp.where(start < N_LAT, start // SEQ, BATCH)


def _mod_spec(chunk, tm):
    return pl.BlockSpec((None, None, 1, D), lambda t, *_: (chunk, _mod_row(t, tm), 0, 0))


def _full_spec(shape):
    nd = len(shape)
    return pl.BlockSpec(shape, lambda *_: (0,) * nd)


def _modulate(x, g, shift, scale):
    ms = jnp.mean(x * x, axis=-1, keepdims=True)
    return (x * lax.rsqrt(ms + EPS) * g) * (1.0 + scale) + shift


def _gelu_tanh(z):
    return 0.5 * z * (1.0 + jnp.tanh(0.7978845608028654 * (z + 0.044715 * (z * z * z))))


def _dot(a, b):
    return jnp.dot(a, b, preferred_element_type=f32)


ADA_TN = 1536


def _ada_kernel(cond_ref, w_ref, b_ref, o_ref):
    cond = cond_ref[...]
    s = (cond * jax.nn.sigmoid(cond)).astype(bf16)
    o_ref[...] = _dot(s, w_ref[...].astype(bf16)) + b_ref[...]


def _ada_all(cond, ada_w, ada_b):
    return pl.pallas_call(
        _ada_kernel,
        out_shape=jax.ShapeDtypeStruct((DEPTH, MOD_ROWS, 6 * D), f32),
        grid=(DEPTH, 6 * D // ADA_TN),
        in_specs=[
            _full_spec((MOD_ROWS, D)),
            pl.BlockSpec((None, D, ADA_TN), lambda i, n: (i, 0, n)),
            pl.BlockSpec((None, 1, ADA_TN), lambda i, n: (i, 0, n)),
        ],
        out_specs=pl.BlockSpec((None, MOD_ROWS, ADA_TN), lambda i, n: (i, 0, n)),
        compiler_params=_params(("parallel", "parallel")),
        name="ada",
    )(cond, ada_w, ada_b.reshape(DEPTH, 1, 6 * D))


GMLP_TM = 256


def _gmlp_kernel(x_ref, sh_ref, sc_ref, gt_ref, g_ref, win_ref, vg_ref, ws_ref, bs_ref, wout_ref,
                 o_ref, uv_ref):
    x = x_ref[...]
    h = _modulate(x, g_ref[...], sh_ref[...], sc_ref[...]).astype(bf16)
    z = _gelu_tanh(_dot(h, win_ref[...]))
    u = z[:, :A_DIM]
    v = z[:, A_DIM:]
    v = (v * lax.rsqrt(jnp.mean(v * v, axis=-1, keepdims=True) + EPS) * vg_ref[...]).astype(bf16)
    for c in range(GMLP_TM // CHUNK):
        r0 = c * CHUNK
        for g in range(A_GROUPS):
            c0 = g * A_GW
            mixed = _dot(ws_ref[g], v[r0:r0 + CHUNK, c0:c0 + A_GW]) + bs_ref[:, c0:c0 + A_GW]
            uv_ref[r0:r0 + CHUNK, c0:c0 + A_GW] = (u[r0:r0 + CHUNK, c0:c0 + A_GW] * mixed).astype(bf16)
    y = _dot(uv_ref[...], wout_ref[...])
    o_ref[...] = x + gt_ref[...] * y


def _gmlp_mixer(x, mods, g, w_in, v_g, w_s, b_full, w_out):
    rows = x.shape[0]
    tm = GMLP_TM
    return pl.pallas_call(
        _gmlp_kernel,
        out_shape=jax.ShapeDtypeStruct((rows, D), f32),
        grid=(rows // tm,),
        in_specs=[
            pl.BlockSpec((tm, D), lambda t: (t, 0)),
            _mod_spec(0, tm), _mod_spec(1, tm), _mod_spec(2, tm),
            _full_spec((1, D)),
            _full_spec((D, 2 * A_DIM)),
            _full_spec((1, A_DIM)),
            _full_spec((A_GROUPS, CHUNK, CHUNK)),
            _full_spec((CHUNK, A_DIM)),
            _full_spec((A_DIM, D)),
        ],
        out_specs=pl.BlockSpec((tm, D), lambda t: (t, 0)),
        scratch_shapes=[pltpu.VMEM((tm, A_DIM), bf16)],
        compiler_params=_params(("parallel",)),
        name="gmlp_mixer",
    )(x, mods, mods, mods, g, w_in, v_g, w_s, b_full, w_out)


FFN_TM = 512
FFN_TF = 1408


def _ffn_kernel(x_ref, sh_ref, sc_ref, gt_ref, g_ref, wg_ref, wu_ref, wd_ref, o_ref, h_ref, acc_ref):
    j = pl.program_id(1)

    @pl.when(j == 0)
    def _():
        h_ref[...] = _modulate(x_ref[...], g_ref[...], sh_ref[...], sc_ref[...]).astype(bf16)
        acc_ref[...] = jnp.zeros_like(acc_ref)

    h = h_ref[...]
    a = _dot(h, wg_ref[...])
    b = _dot(h, wu_ref[...])
    act = (a * jax.nn.sigmoid(a) * b).astype(bf16)
    acc_ref[...] += _dot(act, wd_ref[...])

    @pl.when(j == pl.num_programs(1) - 1)
    def _():
        o_ref[...] = x_ref[...] + gt_ref[...] * acc_ref[...]


def _dense_ffn(x, mods, g, wg, wu, wd):
    rows = x.shape[0]
    tm, tf = FFN_TM, FFN_TF
    return pl.pallas_call(
        _ffn_kernel,
        out_shape=jax.ShapeDtypeStruct((rows, D), f32),
        grid=(rows // tm, FFN_DIM // tf),
        in_specs=[
            pl.BlockSpec((tm, D), lambda t, j: (t, 0)),
            _mod_spec(3, tm), _mod_spec(4, tm), _mod_spec(5, tm),
            _full_spec((1, D)),
            pl.BlockSpec((D, tf), lambda t, j: (0, j)),
            pl.BlockSpec((D, tf), lambda t, j: (0, j)),
            pl.BlockSpec((tf, D), lambda t, j: (j, 0)),
        ],
        out_specs=pl.BlockSpec((tm, D), lambda t, j: (t, 0)),
        scratch_shapes=[pltpu.VMEM((tm, D), bf16), pltpu.VMEM((tm, D), f32)],
        compiler_params=_params(("parallel", "arbitrary")),
        name="dense_ffn",
    )(x, mods, mods, mods, g, wg, wu, wd)


QKV_TM = 512
HEADS_PER_VREG = LANES // HEAD_DIM


def _qkv_kernel(x_ref, sh_ref, sc_ref, g_ref, w_ref, gain_ref, bd_ref, cos_ref, sin_ref, o_ref):
    h = _modulate(x_ref[...], g_ref[...], sh_ref[...], sc_ref[...]).astype(bf16)
    qkv = _dot(h, w_ref[...])
    lane = lax.broadcasted_iota(jnp.int32, (QKV_TM, LANES), 1)
    first_half = (lane % HEAD_DIM) < (HEAD_DIM // 2)
    cos = cos_ref[...]
    sin = sin_ref[...]
    bd = bd_ref[...]
    for cb in range(QKV_COLS // LANES):
        c0 = cb * LANES
        blk = qkv[:, c0:c0 + LANES]
        if c0 < QK_COLS:
            sq = blk * blk
            hi = sq.astype(bf16)
            lo = (sq - hi.astype(f32)).astype(bf16)
            ms = (_dot(hi, bd) + _dot(lo, bd)) * (1.0 / HEAD_DIM)
            y = blk * lax.rsqrt(ms + EPS) * gain_ref[:, c0:c0 + LANES]
            partner = jnp.where(first_half, pltpu.roll(y, LANES - HEAD_DIM // 2, 1),
                                pltpu.roll(y, HEAD_DIM // 2, 1))
            y = y * cos + partner * sin
            if c0 < N_HEADS * HEAD_DIM:
                y = y * (HEAD_DIM ** -0.5)
            blk = y
        o_ref[:, c0:c0 + LANES] = blk.astype(bf16)


def _qkv_proj(x_all, mods, g, w_qkv, gain, bd, cos_t, sin_t):
    tm = QKV_TM
    tiles_per_seq = SEQ // tm

    def tab_idx(t):
        return (jnp.where(t * tm < N_LAT, t % tiles_per_seq, tiles_per_seq), 0)

    return pl.pallas_call(
        _qkv_kernel,
        out_shape=jax.ShapeDtypeStruct((N_ALL, QKV_COLS), bf16),
        grid=(N_ALL // tm,),
        in_specs=[
            pl.BlockSpec((tm, D), lambda t: (t, 0)),
            _mod_spec(0, tm), _mod_spec(1, tm),
            _full_spec((1, D)),
            _full_spec((D, QKV_COLS)),
            _full_spec((1, QK_COLS)),
            _full_spec((LANES, LANES)),
            pl.BlockSpec((tm, LANES), tab_idx),
            pl.BlockSpec((tm, LANES), tab_idx),
        ],
        out_specs=pl.BlockSpec((tm, QKV_COLS), lambda t: (t, 0)),
        compiler_params=_params(("parallel",)),
        name="qkv_proj",
    )(x_all, mods, mods, g, w_qkv, gain, bd, cos_t, sin_t)


ATT_TQ = 256
REP = N_HEADS // N_KV


def _attn_kernel(q_ref, kk_ref, vv_ref, o_ref):
    lane = lax.broadcasted_iota(jnp.int32, (ATT_TQ, LANES), 1)
    low = lane < HEAD_DIM
    kk = kk_ref[...]
    vv = vv_ref[...]
    zero = jnp.zeros((ATT_TQ, LANES), bf16)
    for p in range(REP // HEADS_PER_VREG):
        q2 = q_ref[:, p * LANES:(p + 1) * LANES]
        outs = []
        for half in range(HEADS_PER_VREG):
            qm = jnp.where(low if half == 0 else ~low, q2, zero)
            s = _dot(qm, kk)
            m = jnp.max(s, axis=-1, keepdims=True)
            e = jnp.exp(s - m)
            l = jnp.sum(e, axis=-1, keepdims=True)
            outs.append(_dot(e.astype(bf16), vv) / l)
        o_ref[:, p * LANES:(p + 1) * LANES] = jnp.where(low, outs[0], outs[1]).astype(bf16)


def _attention(qkv, kk, vv):
    tq = ATT_TQ
    qt_per_seq = SEQ // tq
    gw = REP * HEAD_DIM
    return pl.pallas_call(
        _attn_kernel,
        out_shape=jax.ShapeDtypeStruct((N_LAT, D), bf16),
        grid=(BATCH, N_KV, qt_per_seq),
        in_specs=[
            pl.BlockSpec((tq, gw), lambda b, g, t: (b * qt_per_seq + t, g)),
            pl.BlockSpec((None, None, LANES, LK), lambda b, g, t: (b, g, 0, 0)),
            pl.BlockSpec((None, None, LK, LANES), lambda b, g, t: (b, g, 0, 0)),
        ],
        out_specs=pl.BlockSpec((tq, gw), lambda b, g, t: (b * qt_per_seq + t, g)),
        compiler_params=_params(("parallel", "parallel", "arbitrary")),
        name="attention",
    )(qkv, kk, vv)


PROJ_TM = 512


def _oproj_kernel(o_ref, x_ref, gt_ref, w_ref, out_ref):
    out_ref[...] = x_ref[...] + gt_ref[...] * _dot(o_ref[...], w_ref[...])


def _attn_out(o, x_all, mods, w_o):
    tm = PROJ_TM
    return pl.pallas_call(
        _oproj_kernel,
        out_shape=jax.ShapeDtypeStruct((N_LAT, D), f32),
        grid=(N_LAT // tm,),
        in_specs=[
            pl.BlockSpec((tm, D), lambda t: (t, 0)),
            pl.BlockSpec((tm, D), lambda t: (t, 0)),
            _mod_spec(2, tm),
            _full_spec((D, D)),
        ],
        out_specs=pl.BlockSpec((tm, D), lambda t: (t, 0)),
        compiler_params=_params(("parallel",)),
        name="attn_out",
    )(o, x_all, mods, w_o)


POOL_TM = 512
POOL_HALO = 128
POOL_EXT = POOL_TM + 2 * POOL_HALO


def _pool_kernel(xp_ref, x_ref, xn_ref, sh_ref, sc_ref, gt_ref, g_ref, band_ref, pw_ref, ps_ref, o_ref):
    t = pl.program_id(0)
    tiles_per_seq = SEQ // POOL_TM
    ts = t % tiles_per_seq
    x = x_ref[...]
    g, sh, sc = g_ref[...], sh_ref[...], sc_ref[...]
    h = _modulate(x, g, sh, sc)
    hp = jnp.where(ts > 0, _modulate(xp_ref[...], g, sh, sc), 0.0)
    hn = jnp.where(ts < tiles_per_seq - 1, _modulate(xn_ref[...], g, sh, sc), 0.0)
    ext = jnp.concatenate([hp, h, hn], axis=0)
    e_hi = ext.astype(bf16)
    r1 = ext - e_hi.astype(f32)
    e_mid = r1.astype(bf16)
    e_lo = (r1 - e_mid.astype(f32)).astype(bf16)
    pos = ts * POOL_TM + lax.broadcasted_iota(jnp.int32, (POOL_TM, 1), 0)
    for j, w in enumerate(POOL_WINDOWS):
        c0 = j * POOL_GROUP
        band = band_ref[j]
        s = (_dot(band, e_hi[:, c0:c0 + POOL_GROUP]) + _dot(band, e_mid[:, c0:c0 + POOL_GROUP])
             + _dot(band, e_lo[:, c0:c0 + POOL_GROUP]))
        lo_i = jnp.maximum(pos - w // 2, 0)
        hi_i = jnp.minimum(pos + w // 2 - 1, SEQ - 1)
        cnt = (hi_i - lo_i + 1).astype(f32)
        d = (s / cnt - h[:, c0:c0 + POOL_GROUP]).astype(bf16)
        y = _dot(d, pw_ref[j]) * ps_ref[:, c0:c0 + POOL_GROUP]
        o_ref[:, c0:c0 + POOL_GROUP] = x[:, c0:c0 + POOL_GROUP] + gt_ref[:, c0:c0 + POOL_GROUP] * y


def _pool_bands():
    t = np.arange(POOL_TM)[:, None] + POOL_HALO
    s = np.arange(POOL_EXT)[None, :]
    bands = [((s >= t - w // 2) & (s <= t + w // 2 - 1)) for w in POOL_WINDOWS]
    return jnp.asarray(np.stack(bands).astype(np.float32), dtype=bf16)


def _pool_mixer(x, mods, g, p_w, p_scale):
    tm = POOL_TM
    hb = tm // POOL_HALO
    n_halo_blocks = N_LAT // POOL_HALO
    return pl.pallas_call(
        _pool_kernel,
        out_shape=jax.ShapeDtypeStruct((N_LAT, D), f32),
        grid=(N_LAT // tm,),
        in_specs=[
            pl.BlockSpec((POOL_HALO, D), lambda t: (jnp.maximum(t * hb - 1, 0), 0)),
            pl.BlockSpec((tm, D), lambda t: (t, 0)),
            pl.BlockSpec((POOL_HALO, D), lambda t: (jnp.minimum((t + 1) * hb, n_halo_blocks - 1), 0)),
            _mod_spec(0, tm), _mod_spec(1, tm), _mod_spec(2, tm),
            _full_spec((1, D)),
            _full_spec((len(POOL_WINDOWS), POOL_TM, POOL_EXT)),
            _full_spec((len(POOL_WINDOWS), POOL_GROUP, POOL_GROUP)),
            _full_spec((1, D)),
        ],
        out_specs=pl.BlockSpec((tm, D), lambda t: (t, 0)),
        compiler_params=_params(("parallel",)),
        name="pool_mixer",
    )(x, x, x, mods, mods, mods, g, _pool_bands(), p_w, p_scale)


ROUTER_TM = 512


def _pack_bf16_halves(a):
    bits = lax.bitcast_convert_type(a.astype(bf16).astype(f32), jnp.uint32)
    half = a.shape[1] // 2
    return (bits[:, half:] & jnp.uint32(0xFFFF0000)) | (bits[:, :half] >> 16)


def _unpack_bf16_halves(p):
    lo = lax.bitcast_convert_type(p << 16, f32)
    hi = lax.bitcast_convert_type(p & jnp.uint32(0xFFFF0000), f32)
    return jnp.concatenate([lo, hi], axis=1)


def _router_kernel(x_ref, sh_ref, sc_ref, g_ref, wr_ref, ltri_ref, hp_ref, route_ref, cnt_ref, run_ref):
    @pl.when(pl.program_id(0) == 0)
    def _():
        run_ref[...] = jnp.zeros_like(run_ref)

    h = _modulate(x_ref[...], g_ref[...], sh_ref[...], sc_ref[...])
    hp_ref[...] = _pack_bf16_halves(h)
    logits = jnp.dot(h, wr_ref[...], preferred_element_type=f32, precision=lax.Precision.HIGHEST)
    lane = lax.broadcasted_iota(jnp.int32, logits.shape, 1)
    neg = jnp.float32(-jnp.inf)
    logits = jnp.where(lane < N_EXPERTS, logits, neg)
    m1 = jnp.max(logits, axis=-1, keepdims=True)
    i1 = jnp.min(jnp.where(logits == m1, lane, LANES), axis=-1, keepdims=True)
    rest = jnp.where(lane == i1, neg, logits)
    m2 = jnp.max(rest, axis=-1, keepdims=True)
    i2 = jnp.min(jnp.where(rest == m2, lane, LANES), axis=-1, keepdims=True)
    e2 = jnp.exp(m2 - m1)
    g1 = 1.0 / (1.0 + e2)
    g2 = e2 / (1.0 + e2)
    oh1 = jnp.where(lane == i1, 1.0, 0.0)
    oh2 = jnp.where(lane == i2, 1.0, 0.0)
    ltri = ltri_ref[...]
    before1 = _dot(ltri, oh1.astype(bf16))
    before2 = _dot(ltri, oh2.astype(bf16))
    tot1 = jnp.sum(oh1, axis=0, keepdims=True)
    tot2 = jnp.sum(oh2, axis=0, keepdims=True)
    run = run_ref[...]
    lp1 = jnp.sum(oh1 * (run + before1), axis=-1, keepdims=True)
    lp2 = jnp.sum(oh2 * (run + tot1 + before2), axis=-1, keepdims=True)
    run = run + tot1 + tot2
    run_ref[...] = run
    cnt_ref[...] = run
    route = jnp.where(lane == 0, i1.astype(f32),
                      jnp.where(lane == 1, i2.astype(f32),
                                jnp.where(lane == 2, g1,
                                          jnp.where(lane == 3, g2,
                                                    jnp.where(lane == 4, lp1, jnp.where(lane == 5, lp2, 0.0))))))
    route_ref[...] = route


def _router(x, mods, g, wr_pad):
    tm = ROUTER_TM
    ltri = jnp.asarray(np.tril(np.ones((tm, tm), np.float32), -1), dtype=bf16)
    return pl.pallas_call(
        _router_kernel,
        out_shape=(jax.ShapeDtypeStruct((N_LAT, D // 2), jnp.uint32),
                   jax.ShapeDtypeStruct((N_LAT, LANES), f32),
                   jax.ShapeDtypeStruct((1, LANES), f32)),
        grid=(N_LAT // tm,),
        in_specs=[
            pl.BlockSpec((tm, D), lambda t: (t, 0)),
            _mod_spec(3, tm), _mod_spec(4, tm),
            _full_spec((1, D)),
            _full_spec((D, LANES)),
            _full_spec((tm, tm)),
        ],
        out_specs=(pl.BlockSpec((tm, D // 2), lambda t: (t, 0)),
                   pl.BlockSpec((tm, LANES), lambda t: (t, 0)),
                   _full_spec((1, LANES))),
        scratch_shapes=[pltpu.VMEM((1, LANES), f32)],
        compiler_params=_params(("arbitrary",)),
        name="router",
    )(x, mods, mods, g, wr_pad, ltri)


MOE_TM = 512
MOE_TF = 896
MOE_TILES = (2 * N_LAT) // MOE_TM + N_EXPERTS
MOE_ROWS = MOE_TILES * MOE_TM


def _expert_kernel(te_ref, nt_ref, xs_ref, wg_ref, wu_ref, wd_ref, y_ref, xb_ref, acc_ref):
    t = pl.program_id(0)
    j = pl.program_id(1)
    last = pl.num_programs(1) - 1
    active = t < nt_ref[0]

    @pl.when(active)
    def _():
        @pl.when(j == 0)
        def _():
            xb_ref[...] = _unpack_bf16_halves(xs_ref[...]).astype(bf16)
            acc_ref[...] = jnp.zeros_like(acc_ref)

        xs = xb_ref[...]
        a = _dot(xs, wg_ref[...])
        b = _dot(xs, wu_ref[...])
        act = (a * jax.nn.sigmoid(a) * b).astype(bf16)
        acc_ref[...] += _dot(act, wd_ref[...])

        @pl.when(j == last)
        def _():
            y_ref[...] = _pack_bf16_halves(acc_ref[...])

    @pl.when(jnp.logical_not(active) & (j == last))
    def _():
        y_ref[...] = jnp.zeros_like(y_ref)


def _experts(tile_expert, num_tiles, xs, wg, wu, wd):
    tm, tf = MOE_TM, MOE_TF
    nj = EXPERT_DIM // tf

    def w_col(t, j, te, nt):
        jj = jnp.where(t < nt[0], j, nj - 1)
        return (te[t], 0, jj)

    def w_row(t, j, te, nt):
        jj = jnp.where(t < nt[0], j, nj - 1)
        return (te[t], jj, 0)

    return pl.pallas_call(
        _expert_kernel,
        out_shape=jax.ShapeDtypeStruct((MOE_ROWS, D // 2), jnp.uint32),
        grid_spec=pltpu.PrefetchScalarGridSpec(
            num_scalar_prefetch=2,
            grid=(MOE_TILES, nj),
            in_specs=[
                pl.BlockSpec((tm, D // 2), lambda t, j, te, nt: (jnp.minimum(t, nt[0] - 1), 0)),
                pl.BlockSpec((None, D, tf), w_col),
                pl.BlockSpec((None, D, tf), w_col),
                pl.BlockSpec((None, tf, D), w_row),
            ],
            out_specs=pl.BlockSpec((tm, D // 2), lambda t, j, te, nt: (t, 0)),
            scratch_shapes=[pltpu.VMEM((tm, D), bf16), pltpu.VMEM((tm, D), f32)],
        ),
        compiler_params=_params(("arbitrary", "arbitrary")),
        name="moe_experts",
    )(tile_expert, num_tiles, xs, wg, wu, wd)


COMBINE_TM = 512


def _combine_kernel(x_ref, y1_ref, y2_ref, route_ref, gt_ref, o_ref):
    route = route_ref[...]
    mix = route[:, 2:3] * _unpack_bf16_halves(y1_ref[...]) + route[:, 3:4] * _unpack_bf16_halves(y2_ref[...])
    o_ref[...] = x_ref[...] + gt_ref[...] * mix


def _combine(x, y1, y2, route, mods):
    tm = COMBINE_TM
    return pl.pallas_call(
        _combine_kernel,
        out_shape=jax.ShapeDtypeStruct((N_LAT, D), f32),
        grid=(N_LAT // tm,),
        in_specs=[
            pl.BlockSpec((tm, D), lambda t: (t, 0)),
            pl.BlockSpec((tm, D // 2), lambda t: (t, 0)),
            pl.BlockSpec((tm, D // 2), lambda t: (t, 0)),
            pl.BlockSpec((tm, LANES), lambda t: (t, 0)),
            _mod_spec(5, tm),
        ],
        out_specs=pl.BlockSpec((tm, D), lambda t: (t, 0)),
        compiler_params=_params(("parallel",)),
        name="moe_combine",
    )(x, y1, y2, route, mods)


def _expert_table(table, e):
    ids = jnp.arange(N_EXPERTS, dtype=jnp.int32)
    return jnp.sum(jnp.where(e[:, None] == ids[None, :], table[None, :], 0), axis=1)


def _take_rows(a, idx):
    return a.at[idx].get(mode="promise_in_bounds")


def _moe(x, mods, g, wr_pad, wg, wu, wd, expert_base):
    tm = MOE_TM
    hp, route, cnt = _router(x, mods, g, wr_pad)
    e1 = route[:, 0].astype(jnp.int32)
    e2 = route[:, 1].astype(jnp.int32)
    counts = cnt[0, :N_EXPERTS].astype(jnp.int32)
    tiles_per_e = (counts + tm - 1) // tm
    tile_end = jnp.cumsum(tiles_per_e)
    group_base = (tile_end - tiles_per_e) * tm
    dense_base = jnp.cumsum(counts) - counts
    pos1 = _expert_table(group_base, e1) + route[:, 4].astype(jnp.int32)
    pos2 = _expert_table(group_base, e2) + route[:, 5].astype(jnp.int32)
    num_tiles = tile_end[-1]
    tile_ids = jnp.arange(MOE_TILES, dtype=jnp.int32)
    tile_expert = jnp.sum(tile_end[None, :] <= jnp.minimum(tile_ids, num_tiles - 1)[:, None], axis=1)
    tile_expert = jnp.minimum(tile_expert, N_EXPERTS - 1).astype(jnp.int32)
    tok = jnp.arange(N_LAT, dtype=jnp.int32)
    _, sorted_tok = lax.sort_key_val(jnp.concatenate([pos1, pos2]), jnp.concatenate([tok, tok]))
    row_expert = jnp.repeat(tile_expert, tm)
    src = jnp.arange(MOE_ROWS, dtype=jnp.int32) - _expert_table(group_base - dense_base, row_expert)
    row_token = _take_rows(sorted_tok, jnp.clip(src, 0, 2 * N_LAT - 1))
    xs = _take_rows(hp, row_token)
    y = _experts(tile_expert + expert_base, num_tiles.reshape(1).astype(jnp.int32), xs, wg, wu, wd)
    return _combine(x, _take_rows(y, pos1), _take_rows(y, pos2), route, mods)


def _rope_tables():
    rows = SEQ // GRID_W
    r = jnp.repeat(jnp.arange(rows), GRID_W)
    col = jnp.tile(jnp.arange(GRID_W), rows)
    inv = ROPE_THETA ** (-jnp.arange(ROPE_FREQS, dtype=f32) / ROPE_FREQS)
    ang = jnp.stack([r, col], axis=-1).astype(f32)[..., None] * inv
    c = jnp.cos(ang).reshape(SEQ, 2 * ROPE_FREQS)
    s = jnp.sin(ang).reshape(SEQ, 2 * ROPE_FREQS)
    cos_h = jnp.concatenate([c, c], axis=-1)
    sin_h = jnp.concatenate([-s, s], axis=-1)
    cos_t = jnp.tile(cos_h, (1, HEADS_PER_VREG))
    sin_t = jnp.tile(sin_h, (1, HEADS_PER_VREG))
    cos_t = jnp.concatenate([cos_t, jnp.ones((QKV_TM, LANES), f32)], axis=0)
    sin_t = jnp.concatenate([sin_t, jnp.zeros((QKV_TM, LANES), f32)], axis=0)
    return cos_t, sin_t


def kernel(x, c, ctx, c_ctx, ada_w, ada_b, norm_g, a_w_in, a_v_g, a_ws, a_bs, a_w_out, b_w_qkv, b_q_g, b_k_g,
           b_w_o, p_w, p_scale, f_w_gate, f_w_up, f_w_down, m_router, m_w_gate, m_w_up, m_w_down):
    cond = jnp.concatenate([c, c_ctx[None, :], jnp.zeros((MOD_ROWS - BATCH - 1, D), f32)], axis=0)
    ada = _ada_all(cond, ada_w, ada_b)
    mods = ada.reshape(DEPTH, MOD_ROWS, 6, 1, D).transpose(0, 2, 1, 3, 4)

    def gmlp_weights(j):
        b_full = jnp.repeat(a_bs[j].T, A_GW, axis=1)
        return (a_w_in[j].astype(bf16), a_v_g[j][None, :], a_ws[j].astype(bf16), b_full,
                a_w_out[j].astype(bf16))

    n_moe = m_w_gate.shape[0]
    moe_wg = m_w_gate.astype(bf16).reshape(n_moe * N_EXPERTS, D, EXPERT_DIM)
    moe_wu = m_w_up.astype(bf16).reshape(n_moe * N_EXPERTS, D, EXPERT_DIM)
    moe_wd = m_w_down.astype(bf16).reshape(n_moe * N_EXPERTS, EXPERT_DIM, D)

    def moe_weights(f):
        wr_pad = jnp.pad(m_router[f], ((0, 0), (0, LANES - N_EXPERTS)))
        return wr_pad, moe_wg, moe_wu, moe_wd, f * N_EXPERTS

    x_all = jnp.concatenate([x.reshape(N_LAT, D), ctx.reshape(N_CTX, D)], axis=0)

    x_all = _gmlp_mixer(x_all, mods[0], norm_g[0, 0][None, :], *gmlp_weights(0))
    x_all = _dense_ffn(x_all, mods[0], norm_g[0, 1][None, :], f_w_gate[0].astype(bf16),
                       f_w_up[0].astype(bf16), f_w_down[0].astype(bf16))

    gain = jnp.concatenate([jnp.tile(b_q_g[0], N_HEADS), jnp.tile(b_k_g[0], N_KV)])[None, :]
    head_of_lane = np.arange(LANES) // HEAD_DIM
    bd = jnp.asarray((head_of_lane[:, None] == head_of_lane[None, :]).astype(np.float32), dtype=bf16)
    cos_t, sin_t = _rope_tables()
    qkv = _qkv_proj(x_all, mods[1], norm_g[1, 0][None, :], b_w_qkv[0].astype(bf16), gain, bd, cos_t, sin_t)

    def keys_first(a):
        return jnp.concatenate([a[N_LAT:].reshape(BATCH, CTX_LEN, -1), a[:N_LAT].reshape(BATCH, SEQ, -1)], axis=1)

    k = keys_first(qkv[:, N_HEADS * HEAD_DIM:QK_COLS]).reshape(BATCH, LK, N_KV, HEAD_DIM)
    v = keys_first(qkv[:, QK_COLS:]).reshape(BATCH, LK, N_KV, HEAD_DIM)
    kt = k.transpose(0, 2, 3, 1)
    kk = jnp.concatenate([kt] * HEADS_PER_VREG, axis=2)
    vt = v.transpose(0, 2, 1, 3)
    vv = jnp.concatenate([vt] * HEADS_PER_VREG, axis=3)
    o = _attention(qkv, kk, vv)
    xl = _attn_out(o, x_all, mods[1], b_w_o[0].astype(bf16))
    xl = _moe(xl, mods[1], norm_g[1, 1][None, :], *moe_weights(0))

    xl = _pool_mixer(xl, mods[2], norm_g[2, 0][None, :], p_w[0].astype(bf16), p_scale[0][None, :])
    xl = _dense_ffn(xl, mods[2], norm_g[2, 1][None, :], f_w_gate[1].astype(bf16),
                    f_w_up[1].astype(bf16), f_w_down[1].astype(bf16))

    xl = _gmlp_mixer(xl, mods[3], norm_g[3, 0][None, :], *gmlp_weights(1))
    xl = _moe(xl, mods[3], norm_g[3, 1][None, :], *moe_weights(1))
    return xl.reshape(BATCH, SEQ, D)
```

```python
import functools

import jax
import jax.numpy as jnp
import numpy as np
from jax import lax
from jax.experimental import pallas as pl
from jax.experimental.pallas import tpu as pltpu

D = 1024
BATCH = 4
SEQ = 4096
DEPTH = 4
GRID_W = 64
CTX_LEN = 256
CHUNK = 128
A_DIM = 2 * D
A_GROUPS = 8
A_GW = A_DIM // A_GROUPS
N_HEADS = 16
N_KV = 4
HEAD_DIM = 64
ROPE_FREQS = HEAD_DIM // 4
ROPE_THETA = 10000.0
POOL_WINDOWS = (2, 4, 8, 16)
POOL_GROUP = D // 4
FFN_DIM = 2816
N_EXPERTS = 8
EXPERT_DIM = 3584
EPS = 1e-6

N_LAT = BATCH * SEQ
N_CTX = BATCH * CTX_LEN
N_ALL = N_LAT + N_CTX
LK = CTX_LEN + SEQ
MOD_ROWS = 8
LANES = 128
QK_COLS = (N_HEADS + N_KV) * HEAD_DIM
QKV_COLS = (N_HEADS + 2 * N_KV) * HEAD_DIM

VMEM_LIMIT = 56 * 1024 * 1024

bf16 = jnp.bfloat16
f32 = jnp.float32


def _params(sem, vmem=VMEM_LIMIT):
    return pltpu.CompilerParams(dimension_semantics=sem, vmem_limit_bytes=vmem)


def _mod_row(t, tm):
    start = t * tm
    return jnp.where(start < N_LAT, start // SEQ, BATCH)


def _mod_spec(chunk, tm):
    return pl.BlockSpec((None, None, 1, D), lambda t, *_: (chunk, _mod_row(t, tm), 0, 0))


def _full_spec(shape):
    nd = len(shape)
    return pl.BlockSpec(shape, lambda *_: (0,) * nd)


def _modulate(x, g, shift, scale):
    ms = jnp.mean(x * x, axis=-1, keepdims=True)
    return (x * lax.rsqrt(ms + EPS) * g) * (1.0 + scale) + shift


def _gelu_tanh(z):
    return 0.5 * z * (1.0 + jnp.tanh(0.7978845608028654 * (z + 0.044715 * (z * z * z))))


def _dot(a, b):
    return jnp.dot(a, b, preferred_element_type=f32)


ADA_TN = 1536


def _ada_kernel(cond_ref, w_ref, b_ref, o_ref):
    cond = cond_ref[...]
    s = (cond * jax.nn.sigmoid(cond)).astype(bf16)
    o_ref[...] = _dot(s, w_ref[...].astype(bf16)) + b_ref[...]


def _ada_all(cond, ada_w, ada_b):
    return pl.pallas_call(
        _ada_kernel,
        out_shape=jax.ShapeDtypeStruct((DEPTH, MOD_ROWS, 6 * D), f32),
        grid=(DEPTH, 6 * D // ADA_TN),
        in_specs=[
            _full_spec((MOD_ROWS, D)),
            pl.BlockSpec((None, D, ADA_TN), lambda i, n: (i, 0, n)),
            pl.BlockSpec((None, 1, ADA_TN), lambda i, n: (i, 0, n)),
        ],
        out_specs=pl.BlockSpec((None, MOD_ROWS, ADA_TN), lambda i, n: (i, 0, n)),
        compiler_params=_params(("parallel", "parallel")),
        name="ada",
    )(cond, ada_w, ada_b.reshape(DEPTH, 1, 6 * D))


GMLP_TM = 256


def _gmlp_kernel(x_ref, sh_ref, sc_ref, gt_ref, g_ref, win_ref, vg_ref, ws_ref, bs_ref, wout_ref,
                 o_ref, uv_ref):
    x = x_ref[...]
    h = _modulate(x, g_ref[...], sh_ref[...], sc_ref[...]).astype(bf16)
    z = _gelu_tanh(_dot(h, win_ref[...]))
    u = z[:, :A_DIM]
    v = z[:, A_DIM:]
    v = (v * lax.rsqrt(jnp.mean(v * v, axis=-1, keepdims=True) + EPS) * vg_ref[...]).astype(bf16)
    for c in range(GMLP_TM // CHUNK):
        r0 = c * CHUNK
        for g in range(A_GROUPS):
            c0 = g * A_GW
            mixed = _dot(ws_ref[g], v[r0:r0 + CHUNK, c0:c0 + A_GW]) + bs_ref[:, c0:c0 + A_GW]
            uv_ref[r0:r0 + CHUNK, c0:c0 + A_GW] = (u[r0:r0 + CHUNK, c0:c0 + A_GW] * mixed).astype(bf16)
    y = _dot(uv_ref[...], wout_ref[...])
    o_ref[...] = x + gt_ref[...] * y


def _gmlp_mixer(x, mods, g, w_in, v_g, w_s, b_full, w_out):
    rows = x.shape[0]
    tm = GMLP_TM
    return pl.pallas_call(
        _gmlp_kernel,
        out_shape=jax.ShapeDtypeStruct((rows, D), f32),
        grid=(rows // tm,),
        in_specs=[
            pl.BlockSpec((tm, D), lambda t: (t, 0)),
            _mod_spec(0, tm), _mod_spec(1, tm), _mod_spec(2, tm),
            _full_spec((1, D)),
            _full_spec((D, 2 * A_DIM)),
            _full_spec((1, A_DIM)),
            _full_spec((A_GROUPS, CHUNK, CHUNK)),
            _full_spec((CHUNK, A_DIM)),
            _full_spec((A_DIM, D)),
        ],
        out_specs=pl.BlockSpec((tm, D), lambda t: (t, 0)),
        scratch_shapes=[pltpu.VMEM((tm, A_DIM), bf16)],
        compiler_params=_params(("parallel",)),
        name="gmlp_mixer",
    )(x, mods, mods, mods, g, w_in, v_g, w_s, b_full, w_out)


FFN_TM = 512


def _ffn_kernel(x_ref, sh_ref, sc_ref, gt_ref, g_ref, wg_ref, wu_ref, wd_ref, o_ref):
    x = x_ref[...]
    h = _modulate(x, g_ref[...], sh_ref[...], sc_ref[...]).astype(bf16)
    a = _dot(h, wg_ref[...])
    b = _dot(h, wu_ref[...])
    act = (a * jax.nn.sigmoid(a) * b).astype(bf16)
    o_ref[...] = x + gt_ref[...] * _dot(act, wd_ref[...])


def _resident_spec(shape, layer):
    nd = len(shape)
    return pl.BlockSpec((None,) + shape, lambda *_: (layer,) + (0,) * nd, pipeline_mode=pl.Buffered(1))


def _dense_ffn(x, mods, g, wg, wu, wd, layer):
    rows = x.shape[0]
    tm = FFN_TM
    return pl.pallas_call(
        _ffn_kernel,
        out_shape=jax.ShapeDtypeStruct((rows, D), f32),
        grid=(rows // tm,),
        in_specs=[
            pl.BlockSpec((tm, D), lambda t: (t, 0)),
            _mod_spec(3, tm), _mod_spec(4, tm), _mod_spec(5, tm),
            _full_spec((1, D)),
            _resident_spec((D, FFN_DIM), layer),
            _resident_spec((D, FFN_DIM), layer),
            _resident_spec((FFN_DIM, D), layer),
        ],
        out_specs=pl.BlockSpec((tm, D), lambda t: (t, 0)),
        compiler_params=_params(("parallel",)),
        name="dense_ffn",
    )(x, mods, mods, mods, g, wg, wu, wd)


QKV_TM = 512
HEADS_PER_VREG = LANES // HEAD_DIM


def _qkv_kernel(x_ref, sh_ref, sc_ref, g_ref, w_ref, gain_ref, bd_ref, cos_ref, sin_ref, o_ref):
    h = _modulate(x_ref[...], g_ref[...], sh_ref[...], sc_ref[...]).astype(bf16)
    qkv = _dot(h, w_ref[...])
    lane = lax.broadcasted_iota(jnp.int32, (QKV_TM, LANES), 1)
    first_half = (lane % HEAD_DIM) < (HEAD_DIM // 2)
    cos = cos_ref[...]
    sin = sin_ref[...]
    bd = bd_ref[...]
    for cb in range(QKV_COLS // LANES):
        c0 = cb * LANES
        blk = qkv[:, c0:c0 + LANES]
        if c0 < QK_COLS:
            sq = blk * blk
            hi = sq.astype(bf16)
            lo = (sq - hi.astype(f32)).astype(bf16)
            ms = (_dot(hi, bd) + _dot(lo, bd)) * (1.0 / HEAD_DIM)
            y = blk * lax.rsqrt(ms + EPS) * gain_ref[:, c0:c0 + LANES]
            partner = jnp.where(first_half, pltpu.roll(y, LANES - HEAD_DIM // 2, 1),
                                pltpu.roll(y, HEAD_DIM // 2, 1))
            y = y * cos + partner * sin
            if c0 < N_HEADS * HEAD_DIM:
                y = y * (HEAD_DIM ** -0.5)
            blk = y
        o_ref[:, c0:c0 + LANES] = blk.astype(bf16)


def _qkv_proj(x_all, mods, g, w_qkv, gain, bd, cos_t, sin_t):
    tm = QKV_TM
    tiles_per_seq = SEQ // tm

    def tab_idx(t):
        return (jnp.where(t * tm < N_LAT, t % tiles_per_seq, tiles_per_seq), 0)

    return pl.pallas_call(
        _qkv_kernel,
        out_shape=jax.ShapeDtypeStruct((N_ALL, QKV_COLS), bf16),
        grid=(N_ALL // tm,),
        in_specs=[
            pl.BlockSpec((tm, D), lambda t: (t, 0)),
            _mod_spec(0, tm), _mod_spec(1, tm),
            _full_spec((1, D)),
            _full_spec((D, QKV_COLS)),
            _full_spec((1, QK_COLS)),
            _full_spec((LANES, LANES)),
            pl.BlockSpec((tm, LANES), tab_idx),
            pl.BlockSpec((tm, LANES), tab_idx),
        ],
        out_specs=pl.BlockSpec((tm, QKV_COLS), lambda t: (t, 0)),
        compiler_params=_params(("parallel",)),
        name="qkv_proj",
    )(x_all, mods, mods, g, w_qkv, gain, bd, cos_t, sin_t)


ATT_TQ = 256
REP = N_HEADS // N_KV


def _attn_kernel(q_ref, kk_ref, vv_ref, o_ref):
    lane = lax.broadcasted_iota(jnp.int32, (ATT_TQ, LANES), 1)
    low = lane < HEAD_DIM
    kk = kk_ref[...]
    vv = vv_ref[...]
    zero = jnp.zeros((ATT_TQ, LANES), bf16)
    for p in range(REP // HEADS_PER_VREG):
        q2 = q_ref[:, p * LANES:(p + 1) * LANES]
        outs = []
        for half in range(HEADS_PER_VREG):
            qm = jnp.where(low if half == 0 else ~low, q2, zero)
            s = _dot(qm, kk)
            m = jnp.max(s, axis=-1, keepdims=True)
            e = jnp.exp(s - m)
            l = jnp.sum(e, axis=-1, keepdims=True)
            outs.append(_dot(e.astype(bf16), vv) / l)
        o_ref[:, p * LANES:(p + 1) * LANES] = jnp.where(low, outs[0], outs[1]).astype(bf16)


def _attention(qkv, kk, vv):
    tq = ATT_TQ
    qt_per_seq = SEQ // tq
    gw = REP * HEAD_DIM
    return pl.pallas_call(
        _attn_kernel,
        out_shape=jax.ShapeDtypeStruct((N_LAT, D), bf16),
        grid=(BATCH, N_KV, qt_per_seq),
        in_specs=[
            pl.BlockSpec((tq, gw), lambda b, g, t: (b * qt_per_seq + t, g)),
            pl.BlockSpec((None, None, LANES, LK), lambda b, g, t: (b, g, 0, 0)),
            pl.BlockSpec((None, None, LK, LANES), lambda b, g, t: (b, g, 0, 0)),
        ],
        out_specs=pl.BlockSpec((tq, gw), lambda b, g, t: (b * qt_per_seq + t, g)),
        compiler_params=_params(("parallel", "parallel", "arbitrary")),
        name="attention",
    )(qkv, kk, vv)


PROJ_TM = 512


def _oproj_kernel(o_ref, x_ref, gt_ref, w_ref, out_ref):
    out_ref[...] = x_ref[...] + gt_ref[...] * _dot(o_ref[...], w_ref[...])


def _attn_out(o, x_all, mods, w_o):
    tm = PROJ_TM
    return pl.pallas_call(
        _oproj_kernel,
        out_shape=jax.ShapeDtypeStruct((N_LAT, D), f32),
        grid=(N_LAT // tm,),
        in_specs=[
            pl.BlockSpec((tm, D), lambda t: (t, 0)),
            pl.BlockSpec((tm, D), lambda t: (t, 0)),
            _mod_spec(2, tm),
            _full_spec((D, D)),
        ],
        out_specs=pl.BlockSpec((tm, D), lambda t: (t, 0)),
        compiler_params=_params(("parallel",)),
        name="attn_out",
    )(o, x_all, mods, w_o)


POOL_TM = 512
POOL_HALO = 128
POOL_EXT = POOL_TM + 2 * POOL_HALO


def _pool_kernel(xp_ref, x_ref, xn_ref, sh_ref, sc_ref, gt_ref, g_ref, band_ref, pw_ref, ps_ref, o_ref):
    t = pl.program_id(0)
    tiles_per_seq = SEQ // POOL_TM
    ts = t % tiles_per_seq
    x = x_ref[...]
    g, sh, sc = g_ref[...], sh_ref[...], sc_ref[...]
    h = _modulate(x, g, sh, sc)
    hp = jnp.where(ts > 0, _modulate(xp_ref[...], g, sh, sc), 0.0)
    hn = jnp.where(ts < tiles_per_seq - 1, _modulate(xn_ref[...], g, sh, sc), 0.0)
    ext = jnp.concatenate([hp, h, hn], axis=0)
    e_hi = ext.astype(bf16)
    r1 = ext - e_hi.astype(f32)
    e_mid = r1.astype(bf16)
    e_lo = (r1 - e_mid.astype(f32)).astype(bf16)
    pos = ts * POOL_TM + lax.broadcasted_iota(jnp.int32, (POOL_TM, 1), 0)
    for j, w in enumerate(POOL_WINDOWS):
        c0 = j * POOL_GROUP
        band = band_ref[j]
        s = (_dot(band, e_hi[:, c0:c0 + POOL_GROUP]) + _dot(band, e_mid[:, c0:c0 + POOL_GROUP])
             + _dot(band, e_lo[:, c0:c0 + POOL_GROUP]))
        lo_i = jnp.maximum(pos - w // 2, 0)
        hi_i = jnp.minimum(pos + w // 2 - 1, SEQ - 1)
        cnt = (hi_i - lo_i + 1).astype(f32)
        d = (s / cnt - h[:, c0:c0 + POOL_GROUP]).astype(bf16)
        y = _dot(d, pw_ref[j]) * ps_ref[:, c0:c0 + POOL_GROUP]
        o_ref[:, c0:c0 + POOL_GROUP] = x[:, c0:c0 + POOL_GROUP] + gt_ref[:, c0:c0 + POOL_GROUP] * y


def _pool_bands():
    t = np.arange(POOL_TM)[:, None] + POOL_HALO
    s = np.arange(POOL_EXT)[None, :]
    bands = [((s >= t - w // 2) & (s <= t + w // 2 - 1)) for w in POOL_WINDOWS]
    return jnp.asarray(np.stack(bands).astype(np.float32), dtype=bf16)


def _pool_mixer(x, mods, g, p_w, p_scale):
    tm = POOL_TM
    hb = tm // POOL_HALO
    n_halo_blocks = N_LAT // POOL_HALO
    return pl.pallas_call(
        _pool_kernel,
        out_shape=jax.ShapeDtypeStruct((N_LAT, D), f32),
        grid=(N_LAT // tm,),
        in_specs=[
            pl.BlockSpec((POOL_HALO, D), lambda t: (jnp.maximum(t * hb - 1, 0), 0)),
            pl.BlockSpec((tm, D), lambda t: (t, 0)),
            pl.BlockSpec((POOL_HALO, D), lambda t: (jnp.minimum((t + 1) * hb, n_halo_blocks - 1), 0)),
            _mod_spec(0, tm), _mod_spec(1, tm), _mod_spec(2, tm),
            _full_spec((1, D)),
            _full_spec((len(POOL_WINDOWS), POOL_TM, POOL_EXT)),
            _full_spec((len(POOL_WINDOWS), POOL_GROUP, POOL_GROUP)),
            _full_spec((1, D)),
        ],
        out_specs=pl.BlockSpec((tm, D), lambda t: (t, 0)),
        compiler_params=_params(("parallel",)),
        name="pool_mixer",
    )(x, x, x, mods, mods, mods, g, _pool_bands(), p_w, p_scale)


ROUTER_TM = 512


def _pack_bf16_halves(a):
    bits = lax.bitcast_convert_type(a.astype(bf16).astype(f32), jnp.uint32)
    half = a.shape[1] // 2
    return (bits[:, half:] & jnp.uint32(0xFFFF0000)) | (bits[:, :half] >> 16)


def _unpack_bf16_halves(p):
    lo = lax.bitcast_convert_type(p << 16, f32)
    hi = lax.bitcast_convert_type(p & jnp.uint32(0xFFFF0000), f32)
    return jnp.concatenate([lo, hi], axis=1)


def _router_kernel(x_ref, sh_ref, sc_ref, g_ref, wr_ref, ltri_ref, hp_ref, route_ref, cnt_ref, run_ref):
    @pl.when(pl.program_id(0) == 0)
    def _():
        run_ref[...] = jnp.zeros_like(run_ref)

    h = _modulate(x_ref[...], g_ref[...], sh_ref[...], sc_ref[...])
    hp_ref[...] = _pack_bf16_halves(h)
    logits = jnp.dot(h, wr_ref[...], preferred_element_type=f32, precision=lax.Precision.HIGHEST)
    lane = lax.broadcasted_iota(jnp.int32, logits.shape, 1)
    neg = jnp.float32(-jnp.inf)
    logits = jnp.where(lane < N_EXPERTS, logits, neg)
    m1 = jnp.max(logits, axis=-1, keepdims=True)
    i1 = jnp.min(jnp.where(logits == m1, lane, LANES), axis=-1, keepdims=True)
    rest = jnp.where(lane == i1, neg, logits)
    m2 = jnp.max(rest, axis=-1, keepdims=True)
    i2 = jnp.min(jnp.where(rest == m2, lane, LANES), axis=-1, keepdims=True)
    e2 = jnp.exp(m2 - m1)
    g1 = 1.0 / (1.0 + e2)
    g2 = e2 / (1.0 + e2)
    oh1 = jnp.where(lane == i1, 1.0, 0.0)
    oh2 = jnp.where(lane == i2, 1.0, 0.0)
    ltri = ltri_ref[...]
    before1 = _dot(ltri, oh1.astype(bf16))
    before2 = _dot(ltri, oh2.astype(bf16))
    tot1 = jnp.sum(oh1, axis=0, keepdims=True)
    tot2 = jnp.sum(oh2, axis=0, keepdims=True)
    run = run_ref[...]
    lp1 = jnp.sum(oh1 * (run + before1), axis=-1, keepdims=True)
    lp2 = jnp.sum(oh2 * (run + tot1 + before2), axis=-1, keepdims=True)
    run = run + tot1 + tot2
    run_ref[...] = run
    cnt_ref[...] = run
    route = jnp.where(lane == 0, i1.astype(f32),
                      jnp.where(lane == 1, i2.astype(f32),
                                jnp.where(lane == 2, g1,
                                          jnp.where(lane == 3, g2,
                                                    jnp.where(lane == 4, lp1, jnp.where(lane == 5, lp2, 0.0))))))
    route_ref[...] = route


def _router(x, mods, g, wr_pad):
    tm = ROUTER_TM
    ltri = jnp.asarray(np.tril(np.ones((tm, tm), np.float32), -1), dtype=bf16)
    return pl.pallas_call(
        _router_kernel,
        out_shape=(jax.ShapeDtypeStruct((N_LAT, D // 2), jnp.uint32),
                   jax.ShapeDtypeStruct((N_LAT, LANES), f32),
                   jax.ShapeDtypeStruct((1, LANES), f32)),
        grid=(N_LAT // tm,),
        in_specs=[
            pl.BlockSpec((tm, D), lambda t: (t, 0)),
            _mod_spec(3, tm), _mod_spec(4, tm),
            _full_spec((1, D)),
            _full_spec((D, LANES)),
            _full_spec((tm, tm)),
        ],
        out_specs=(pl.BlockSpec((tm, D // 2), lambda t: (t, 0)),
                   pl.BlockSpec((tm, LANES), lambda t: (t, 0)),
                   _full_spec((1, LANES))),
        scratch_shapes=[pltpu.VMEM((1, LANES), f32)],
        compiler_params=_params(("arbitrary",)),
        name="router",
    )(x, mods, mods, g, wr_pad, ltri)


MOE_TM = 512
MOE_TF = 1792
MOE_TILES = (2 * N_LAT) // MOE_TM + N_EXPERTS
MOE_ROWS = MOE_TILES * MOE_TM


def _expert_kernel(te_ref, nt_ref, xs_ref, wg_ref, wu_ref, wd_ref, y_ref, xb_ref, acc_ref):
    t = pl.program_id(0)
    j = pl.program_id(1)
    last = pl.num_programs(1) - 1
    active = t < nt_ref[0]

    @pl.when(active)
    def _():
        @pl.when(j == 0)
        def _():
            xb_ref[...] = _unpack_bf16_halves(xs_ref[...]).astype(bf16)
            acc_ref[...] = jnp.zeros_like(acc_ref)

        xs = xb_ref[...]
        a = _dot(xs, wg_ref[...])
        b = _dot(xs, wu_ref[...])
        act = (a * jax.nn.sigmoid(a) * b).astype(bf16)
        acc_ref[...] += _dot(act, wd_ref[...])

        @pl.when(j == last)
        def _():
            y_ref[...] = _pack_bf16_halves(acc_ref[...])

    @pl.when(jnp.logical_not(active) & (j == last))
    def _():
        y_ref[...] = jnp.zeros_like(y_ref)


def _experts(tile_expert, num_tiles, xs, wg, wu, wd):
    tm, tf = MOE_TM, MOE_TF
    nj = EXPERT_DIM // tf

    def w_col(t, j, te, nt):
        jj = jnp.where(t < nt[0], j, nj - 1)
        return (te[t], 0, jj)

    def w_row(t, j, te, nt):
        jj = jnp.where(t < nt[0], j, nj - 1)
        return (te[t], jj, 0)

    return pl.pallas_call(
        _expert_kernel,
        out_shape=jax.ShapeDtypeStruct((MOE_ROWS, D // 2), jnp.uint32),
        grid_spec=pltpu.PrefetchScalarGridSpec(
            num_scalar_prefetch=2,
            grid=(MOE_TILES, nj),
            in_specs=[
                pl.BlockSpec((tm, D // 2), lambda t, j, te, nt: (jnp.minimum(t, jnp.maximum(nt[0] - 1, 0)), 0)),
                pl.BlockSpec((None, D, tf), w_col),
                pl.BlockSpec((None, D, tf), w_col),
                pl.BlockSpec((None, tf, D), w_row),
            ],
            out_specs=pl.BlockSpec((tm, D // 2), lambda t, j, te, nt: (t, 0)),
            scratch_shapes=[pltpu.VMEM((tm, D), bf16), pltpu.VMEM((tm, D), f32)],
        ),
        compiler_params=_params(("arbitrary", "arbitrary")),
        name="moe_experts",
    )(tile_expert, num_tiles, xs, wg, wu, wd)


COMBINE_TM = 512


def _combine_kernel(x_ref, y1_ref, y2_ref, route_ref, gt_ref, o_ref):
    route = route_ref[...]
    mix = route[:, 2:3] * _unpack_bf16_halves(y1_ref[...]) + route[:, 3:4] * _unpack_bf16_halves(y2_ref[...])
    o_ref[...] = x_ref[...] + gt_ref[...] * mix


def _combine(x, y1, y2, route, mods):
    tm = COMBINE_TM
    return pl.pallas_call(
        _combine_kernel,
        out_shape=jax.ShapeDtypeStruct((N_LAT, D), f32),
        grid=(N_LAT // tm,),
        in_specs=[
            pl.BlockSpec((tm, D), lambda t: (t, 0)),
            pl.BlockSpec((tm, D // 2), lambda t: (t, 0)),
            pl.BlockSpec((tm, D // 2), lambda t: (t, 0)),
            pl.BlockSpec((tm, LANES), lambda t: (t, 0)),
            _mod_spec(5, tm),
        ],
        out_specs=pl.BlockSpec((tm, D), lambda t: (t, 0)),
        compiler_params=_params(("parallel",)),
        name="moe_combine",
    )(x, y1, y2, route, mods)


def _expert_table(table, e):
    ids = jnp.arange(N_EXPERTS, dtype=jnp.int32)
    return jnp.sum(jnp.where(e[:, None] == ids[None, :], table[None, :], 0), axis=1)


def _take_rows(a, idx):
    return a.at[idx].get(mode="promise_in_bounds")


def _moe(x, mods, g, wr_pad, wg, wu, wd, expert_base):
    tm = MOE_TM
    hp, route, cnt = _router(x, mods, g, wr_pad)
    e1 = route[:, 0].astype(jnp.int32)
    e2 = route[:, 1].astype(jnp.int32)
    counts = cnt[0, :N_EXPERTS].astype(jnp.int32)
    tiles_per_e = (counts + tm - 1) // tm
    tile_end = jnp.cumsum(tiles_per_e)
    group_base = (tile_end - tiles_per_e) * tm
    dense_base = jnp.cumsum(counts) - counts
    pos1 = _expert_table(group_base, e1) + route[:, 4].astype(jnp.int32)
    pos2 = _expert_table(group_base, e2) + route[:, 5].astype(jnp.int32)
    num_tiles = tile_end[-1]
    tile_ids = jnp.arange(MOE_TILES, dtype=jnp.int32)
    tile_expert = jnp.sum(tile_end[None, :] <= jnp.minimum(tile_ids, num_tiles - 1)[:, None], axis=1)
    tile_expert = jnp.minimum(tile_expert, N_EXPERTS - 1).astype(jnp.int32)
    tok = jnp.arange(N_LAT, dtype=jnp.int32)
    _, sorted_tok = lax.sort_key_val(jnp.concatenate([pos1, pos2]), jnp.concatenate([tok, tok]))
    row_expert = jnp.repeat(tile_expert, tm)
    src = jnp.arange(MOE_ROWS, dtype=jnp.int32) - _expert_table(group_base - dense_base, row_expert)
    row_token = _take_rows(sorted_tok, jnp.clip(src, 0, 2 * N_LAT - 1))
    xs = _take_rows(hp, row_token)
    y = _experts(tile_expert + expert_base, num_tiles.reshape(1).astype(jnp.int32), xs, wg, wu, wd)
    return _combine(x, _take_rows(y, pos1), _take_rows(y, pos2), route, mods)


def _rope_tables():
    rows = SEQ // GRID_W
    r = jnp.repeat(jnp.arange(rows), GRID_W)
    col = jnp.tile(jnp.arange(GRID_W), rows)
    inv = ROPE_THETA ** (-jnp.arange(ROPE_FREQS, dtype=f32) / ROPE_FREQS)
    ang = jnp.stack([r, col], axis=-1).astype(f32)[..., None] * inv
    c = jnp.cos(ang).reshape(SEQ, 2 * ROPE_FREQS)
    s = jnp.sin(ang).reshape(SEQ, 2 * ROPE_FREQS)
    cos_h = jnp.concatenate([c, c], axis=-1)
    sin_h = jnp.concatenate([-s, s], axis=-1)
    cos_t = jnp.tile(cos_h, (1, HEADS_PER_VREG))
    sin_t = jnp.tile(sin_h, (1, HEADS_PER_VREG))
    cos_t = jnp.concatenate([cos_t, jnp.ones((QKV_TM, LANES), f32)], axis=0)
    sin_t = jnp.concatenate([sin_t, jnp.zeros((QKV_TM, LANES), f32)], axis=0)
    return cos_t, sin_t


def kernel(x, c, ctx, c_ctx, ada_w, ada_b, norm_g, a_w_in, a_v_g, a_ws, a_bs, a_w_out, b_w_qkv, b_q_g, b_k_g,
           b_w_o, p_w, p_scale, f_w_gate, f_w_up, f_w_down, m_router, m_w_gate, m_w_up, m_w_down):
    cond = jnp.concatenate([c, c_ctx[None, :], jnp.zeros((MOD_ROWS - BATCH - 1, D), f32)], axis=0)
    ada = _ada_all(cond, ada_w, ada_b)
    mods = ada.reshape(DEPTH, MOD_ROWS, 6, 1, D).transpose(0, 2, 1, 3, 4)

    def gmlp_weights(j):
        b_full = jnp.repeat(a_bs[j].T, A_GW, axis=1)
        return (a_w_in[j].astype(bf16), a_v_g[j][None, :], a_ws[j].astype(bf16), b_full,
                a_w_out[j].astype(bf16))

    n_moe = m_w_gate.shape[0]
    moe_wg = m_w_gate.astype(bf16).reshape(n_moe * N_EXPERTS, D, EXPERT_DIM)
    moe_wu = m_w_up.astype(bf16).reshape(n_moe * N_EXPERTS, D, EXPERT_DIM)
    moe_wd = m_w_down.astype(bf16).reshape(n_moe * N_EXPERTS, EXPERT_DIM, D)

    def moe_weights(f):
        wr_pad = jnp.pad(m_router[f], ((0, 0), (0, LANES - N_EXPERTS)))
        return wr_pad, moe_wg, moe_wu, moe_wd, f * N_EXPERTS

    x_all = jnp.concatenate([x.reshape(N_LAT, D), ctx.reshape(N_CTX, D)], axis=0)

    x_all = _gmlp_mixer(x_all, mods[0], norm_g[0, 0][None, :], *gmlp_weights(0))
    ffn_w = (f_w_gate.astype(bf16), f_w_up.astype(bf16), f_w_down.astype(bf16))
    x_all = _dense_ffn(x_all, mods[0], norm_g[0, 1][None, :], *ffn_w, 0)

    gain = jnp.concatenate([jnp.tile(b_q_g[0], N_HEADS), jnp.tile(b_k_g[0], N_KV)])[None, :]
    head_of_lane = np.arange(LANES) // HEAD_DIM
    bd = jnp.asarray((head_of_lane[:, None] == head_of_lane[None, :]).astype(np.float32), dtype=bf16)
    cos_t, sin_t = _rope_tables()
    qkv = _qkv_proj(x_all, mods[1], norm_g[1, 0][None, :], b_w_qkv[0].astype(bf16), gain, bd, cos_t, sin_t)

    def keys_first(a):
        return jnp.concatenate([a[N_LAT:].reshape(BATCH, CTX_LEN, -1), a[:N_LAT].reshape(BATCH, SEQ, -1)], axis=1)

    k = keys_first(qkv[:, N_HEADS * HEAD_DIM:QK_COLS]).reshape(BATCH, LK, N_KV, HEAD_DIM)
    v = keys_first(qkv[:, QK_COLS:]).reshape(BATCH, LK, N_KV, HEAD_DIM)
    kt = k.transpose(0, 2, 3, 1)
    kk = jnp.concatenate([kt] * HEADS_PER_VREG, axis=2)
    vt = v.transpose(0, 2, 1, 3)
    vv = jnp.concatenate([vt] * HEADS_PER_VREG, axis=3)
    o = _attention(qkv, kk, vv)
    xl = _attn_out(o, x_all, mods[1], b_w_o[0].astype(bf16))
    xl = _moe(xl, mods[1], norm_g[1, 1][None, :], *moe_weights(0))

    xl = _pool_mixer(xl, mods[2], norm_g[2, 0][None, :], p_w[0].astype(bf16), p_scale[0][None, :])
    xl = _dense_ffn(xl, mods[2], norm_g[2, 1][None, :], *ffn_w, 1)

    xl = _gmlp_mixer(xl, mods[3], norm_g[3, 0][None, :], *gmlp_weights(1))
    xl = _moe(xl, mods[3], norm_g[3, 1][None, :], *moe_weights(1))
    return xl.reshape(BATCH, SEQ, D)
```

```python
import functools

import jax
import jax.numpy as jnp
import numpy as np
from jax import lax
from jax.experimental import pallas as pl
from jax.experimental.pallas import tpu as pltpu

D = 1024
BATCH = 4
SEQ = 4096
DEPTH = 4
GRID_W = 64
CTX_LEN = 256
CHUNK = 128
A_DIM = 2 * D
A_GROUPS = 8
A_GW = A_DIM // A_GROUPS
N_HEADS = 16
N_KV = 4
HEAD_DIM = 64
ROPE_FREQS = HEAD_DIM // 4
ROPE_THETA = 10000.0
POOL_WINDOWS = (2, 4, 8, 16)
POOL_GROUP = D // 4
FFN_DIM = 2816
N_EXPERTS = 8
EXPERT_DIM = 3584
EPS = 1e-6

N_LAT = BATCH * SEQ
N_CTX = BATCH * CTX_LEN
N_ALL = N_LAT + N_CTX
LK = CTX_LEN + SEQ
MOD_ROWS = 8
LANES = 128
QK_COLS = (N_HEADS + N_KV) * HEAD_DIM
QKV_COLS = (N_HEADS + 2 * N_KV) * HEAD_DIM

VMEM_LIMIT = 56 * 1024 * 1024

bf16 = jnp.bfloat16
f32 = jnp.float32


def _params(sem, vmem=VMEM_LIMIT):
    return pltpu.CompilerParams(dimension_semantics=sem, vmem_limit_bytes=vmem)


def _mod_row(t, tm):
    start = t * tm
    return jnp.where(start < N_LAT, start // SEQ, BATCH)


def _mod_spec(chunk, tm):
    return pl.BlockSpec((None, None, 1, D), lambda t, *_: (chunk, _mod_row(t, tm), 0, 0))


def _full_spec(shape):
    nd = len(shape)
    return pl.BlockSpec(shape, lambda *_: (0,) * nd)


def _modulate(x, g, shift, scale):
    ms = jnp.mean(x * x, axis=-1, keepdims=True)
    return (x * lax.rsqrt(ms + EPS) * g) * (1.0 + scale) + shift


def _gelu_tanh(z):
    return 0.5 * z * (1.0 + jnp.tanh(0.7978845608028654 * (z + 0.044715 * (z * z * z))))


def _dot(a, b):
    return jnp.dot(a, b, preferred_element_type=f32)


ADA_TN = 1536


def _ada_kernel(cond_ref, w_ref, b_ref, o_ref):
    cond = cond_ref[...]
    s = (cond * jax.nn.sigmoid(cond)).astype(bf16)
    o_ref[...] = _dot(s, w_ref[...].astype(bf16)) + b_ref[...]


def _ada_all(cond, ada_w, ada_b):
    return pl.pallas_call(
        _ada_kernel,
        out_shape=jax.ShapeDtypeStruct((DEPTH, MOD_ROWS, 6 * D), f32),
        grid=(DEPTH, 6 * D // ADA_TN),
        in_specs=[
            _full_spec((MOD_ROWS, D)),
            pl.BlockSpec((None, D, ADA_TN), lambda i, n: (i, 0, n)),
            pl.BlockSpec((None, 1, ADA_TN), lambda i, n: (i, 0, n)),
        ],
        out_specs=pl.BlockSpec((None, MOD_ROWS, ADA_TN), lambda i, n: (i, 0, n)),
        compiler_params=_params(("parallel", "parallel")),
        name="ada",
    )(cond, ada_w, ada_b.reshape(DEPTH, 1, 6 * D))


GMLP_TM = 256


def _gmlp_kernel(x_ref, sh_ref, sc_ref, gt_ref, g_ref, win_ref, vg_ref, ws_ref, bs_ref, wout_ref,
                 o_ref, uv_ref):
    x = x_ref[...]
    h = _modulate(x, g_ref[...], sh_ref[...], sc_ref[...]).astype(bf16)
    z = _gelu_tanh(_dot(h, win_ref[...]))
    u = z[:, :A_DIM]
    v = z[:, A_DIM:]
    v = (v * lax.rsqrt(jnp.mean(v * v, axis=-1, keepdims=True) + EPS) * vg_ref[...]).astype(bf16)
    for c in range(GMLP_TM // CHUNK):
        r0 = c * CHUNK
        for g in range(A_GROUPS):
            c0 = g * A_GW
            mixed = _dot(ws_ref[g], v[r0:r0 + CHUNK, c0:c0 + A_GW]) + bs_ref[:, c0:c0 + A_GW]
            uv_ref[r0:r0 + CHUNK, c0:c0 + A_GW] = (u[r0:r0 + CHUNK, c0:c0 + A_GW] * mixed).astype(bf16)
    y = _dot(uv_ref[...], wout_ref[...])
    o_ref[...] = x + gt_ref[...] * y


def _gmlp_mixer(x, mods, g, w_in, v_g, w_s, b_full, w_out):
    rows = x.shape[0]
    tm = GMLP_TM
    return pl.pallas_call(
        _gmlp_kernel,
        out_shape=jax.ShapeDtypeStruct((rows, D), f32),
        grid=(rows // tm,),
        in_specs=[
            pl.BlockSpec((tm, D), lambda t: (t, 0)),
            _mod_spec(0, tm), _mod_spec(1, tm), _mod_spec(2, tm),
            _full_spec((1, D)),
            _full_spec((D, 2 * A_DIM)),
            _full_spec((1, A_DIM)),
            _full_spec((A_GROUPS, CHUNK, CHUNK)),
            _full_spec((CHUNK, A_DIM)),
            _full_spec((A_DIM, D)),
        ],
        out_specs=pl.BlockSpec((tm, D), lambda t: (t, 0)),
        scratch_shapes=[pltpu.VMEM((tm, A_DIM), bf16)],
        compiler_params=_params(("parallel",)),
        name="gmlp_mixer",
    )(x, mods, mods, mods, g, w_in, v_g, w_s, b_full, w_out)


FFN_TM = 512


def _ffn_kernel(x_ref, sh_ref, sc_ref, gt_ref, g_ref, wg_ref, wu_ref, wd_ref, o_ref):
    x = x_ref[...]
    h = _modulate(x, g_ref[...], sh_ref[...], sc_ref[...]).astype(bf16)
    a = _dot(h, wg_ref[...])
    b = _dot(h, wu_ref[...])
    act = (a * jax.nn.sigmoid(a) * b).astype(bf16)
    o_ref[...] = x + gt_ref[...] * _dot(act, wd_ref[...])


def _resident_spec(shape, layer):
    nd = len(shape)
    return pl.BlockSpec((None,) + shape, lambda *_: (layer,) + (0,) * nd, pipeline_mode=pl.Buffered(1))


def _dense_ffn(x, mods, g, wg, wu, wd, layer):
    rows = x.shape[0]
    tm = FFN_TM
    return pl.pallas_call(
        _ffn_kernel,
        out_shape=jax.ShapeDtypeStruct((rows, D), f32),
        grid=(rows // tm,),
        in_specs=[
            pl.BlockSpec((tm, D), lambda t: (t, 0)),
            _mod_spec(3, tm), _mod_spec(4, tm), _mod_spec(5, tm),
            _full_spec((1, D)),
            _resident_spec((D, FFN_DIM), layer),
            _resident_spec((D, FFN_DIM), layer),
            _resident_spec((FFN_DIM, D), layer),
        ],
        out_specs=pl.BlockSpec((tm, D), lambda t: (t, 0)),
        compiler_params=_params(("parallel",)),
        name="dense_ffn",
    )(x, mods, mods, mods, g, wg, wu, wd)


QKV_TM = 512
HEADS_PER_VREG = LANES // HEAD_DIM
Q_SCALE = HEAD_DIM ** -0.5 * 1.4426950408889634


def _qkv_kernel(x_ref, sh_ref, sc_ref, g_ref, w_ref, gain_ref, bd_ref, cos_ref, sin_ref, o_ref):
    h = _modulate(x_ref[...], g_ref[...], sh_ref[...], sc_ref[...]).astype(bf16)
    qkv = _dot(h, w_ref[...])
    lane = lax.broadcasted_iota(jnp.int32, (QKV_TM, LANES), 1)
    first_half = (lane % HEAD_DIM) < (HEAD_DIM // 2)
    cos = cos_ref[...]
    sin = sin_ref[...]
    bd = bd_ref[...]
    for cb in range(QKV_COLS // LANES):
        c0 = cb * LANES
        blk = qkv[:, c0:c0 + LANES]
        if c0 < QK_COLS:
            sq = blk * blk
            hi = sq.astype(bf16)
            lo = (sq - hi.astype(f32)).astype(bf16)
            ms = (_dot(hi, bd) + _dot(lo, bd)) * (1.0 / HEAD_DIM)
            y = blk * lax.rsqrt(ms + EPS) * gain_ref[:, c0:c0 + LANES]
            partner = jnp.where(first_half, pltpu.roll(y, LANES - HEAD_DIM // 2, 1),
                                pltpu.roll(y, HEAD_DIM // 2, 1))
            y = y * cos + partner * sin
            if c0 < N_HEADS * HEAD_DIM:
                y = y * Q_SCALE
            blk = y
        o_ref[:, c0:c0 + LANES] = blk.astype(bf16)


def _qkv_proj(x_all, mods, g, w_qkv, gain, bd, cos_t, sin_t):
    tm = QKV_TM
    tiles_per_seq = SEQ // tm

    def tab_idx(t):
        return (jnp.where(t * tm < N_LAT, t % tiles_per_seq, tiles_per_seq), 0)

    return pl.pallas_call(
        _qkv_kernel,
        out_shape=jax.ShapeDtypeStruct((N_ALL, QKV_COLS), bf16),
        grid=(N_ALL // tm,),
        in_specs=[
            pl.BlockSpec((tm, D), lambda t: (t, 0)),
            _mod_spec(0, tm), _mod_spec(1, tm),
            _full_spec((1, D)),
            _full_spec((D, QKV_COLS)),
            _full_spec((1, QK_COLS)),
            _full_spec((LANES, LANES)),
            pl.BlockSpec((tm, LANES), tab_idx),
            pl.BlockSpec((tm, LANES), tab_idx),
        ],
        out_specs=pl.BlockSpec((tm, QKV_COLS), lambda t: (t, 0)),
        compiler_params=_params(("parallel",)),
        name="qkv_proj",
    )(x_all, mods, mods, g, w_qkv, gain, bd, cos_t, sin_t)


ATT_TQ = 512
REP = N_HEADS // N_KV
ATT_KC = 256
ATT_LAG = 2


def _attn_kernel(qt_ref, kk_ref, vt_ref, o_ref, *st_refs):
    row = lax.broadcasted_iota(jnp.int32, (LANES, ATT_TQ), 0)
    low = row < HEAD_DIM
    zero = jnp.zeros((LANES, ATT_TQ), bf16)
    n_chunks = LK // ATT_KC
    sub = 8
    n_buf = len(st_refs)

    def masked_q(h):
        q2 = qt_ref[(h // HEADS_PER_VREG) * LANES:(h // HEADS_PER_VREG + 1) * LANES, :]
        return jnp.where(low if h % HEADS_PER_VREG == 0 else ~low, q2, zero)

    col_max = {}
    for ph in range(REP + ATT_LAG):
        ha, hb = ph, ph - ATT_LAG
        qm = masked_q(ha) if ha < REP else None
        mx = None
        acc = None
        for c in range(n_chunks):
            rows = slice(c * ATT_KC, (c + 1) * ATT_KC)
            if ha < REP:
                st = _dot(kk_ref[rows, :], qm)
                st_refs[ha % n_buf][rows, :] = st
                cm = jnp.max(st.reshape(ATT_KC // sub, sub, ATT_TQ), axis=0)
                mx = cm if mx is None else jnp.maximum(mx, cm)
            if hb >= 0:
                pt = jnp.exp2(st_refs[hb % n_buf][rows, :] - col_max[hb]).astype(bf16)
                part = _dot(vt_ref[:, rows], pt)
                acc = part if acc is None else acc + part
        if hb >= 0:
            o_ref[hb * HEAD_DIM:(hb + 1) * HEAD_DIM, :] = (
                acc[:HEAD_DIM] / acc[HEAD_DIM:HEAD_DIM + 1]).astype(bf16)
        if ha < REP:
            col_max[ha] = jnp.max(mx, axis=0, keepdims=True)


def _attention(qt, kk, vt):
    tq = ATT_TQ
    qt_per_seq = SEQ // tq
    gw = REP * HEAD_DIM
    return pl.pallas_call(
        _attn_kernel,
        out_shape=jax.ShapeDtypeStruct((D, N_LAT), bf16),
        grid=(BATCH, N_KV, qt_per_seq),
        in_specs=[
            pl.BlockSpec((gw, tq), lambda b, g, t: (g, b * qt_per_seq + t)),
            pl.BlockSpec((None, None, LK, LANES), lambda b, g, t: (b, g, 0, 0)),
            pl.BlockSpec((None, None, LANES, LK), lambda b, g, t: (b, g, 0, 0)),
        ],
        out_specs=pl.BlockSpec((gw, tq), lambda b, g, t: (g, b * qt_per_seq + t)),
        scratch_shapes=[pltpu.VMEM((LK, tq), f32)] * (ATT_LAG + 1),
        compiler_params=_params(("parallel", "parallel", "arbitrary")),
        name="attention",
    )(qt, kk, vt)


PROJ_TM = 512


def _oproj_kernel(o_ref, x_ref, gt_ref, w_ref, out_ref):
    out_ref[...] = x_ref[...] + gt_ref[...] * _dot(o_ref[...], w_ref[...])


def _attn_out(o, x_all, mods, w_o):
    tm = PROJ_TM
    return pl.pallas_call(
        _oproj_kernel,
        out_shape=jax.ShapeDtypeStruct((N_LAT, D), f32),
        grid=(N_LAT // tm,),
        in_specs=[
            pl.BlockSpec((tm, D), lambda t: (t, 0)),
            pl.BlockSpec((tm, D), lambda t: (t, 0)),
            _mod_spec(2, tm),
            _full_spec((D, D)),
        ],
        out_specs=pl.BlockSpec((tm, D), lambda t: (t, 0)),
        compiler_params=_params(("parallel",)),
        name="attn_out",
    )(o, x_all, mods, w_o)


POOL_TM = 512
POOL_HALO = 8
POOL_EXT = POOL_TM + 2 * POOL_HALO
assert max(POOL_WINDOWS) // 2 <= POOL_HALO


def _pool_kernel(xp_ref, x_ref, xn_ref, sh_ref, sc_ref, gt_ref, g_ref, pw_ref, ps_ref, o_ref):
    t = pl.program_id(0)
    tiles_per_seq = SEQ // POOL_TM
    ts = t % tiles_per_seq
    x = x_ref[...]
    g, sh, sc = g_ref[...], sh_ref[...], sc_ref[...]
    h = _modulate(x, g, sh, sc)
    hp = jnp.where(ts > 0, _modulate(xp_ref[...], g, sh, sc), 0.0)
    hn = jnp.where(ts < tiles_per_seq - 1, _modulate(xn_ref[...], g, sh, sc), 0.0)
    ext = jnp.concatenate([hp, h, hn], axis=0)
    pos = ts * POOL_TM + lax.broadcasted_iota(jnp.int32, (POOL_TM, 1), 0)
    for j, w in enumerate(POOL_WINDOWS):
        c0 = j * POOL_GROUP
        p = ext[:, c0:c0 + POOL_GROUP]
        k = 1
        while k < w:
            p = p + pltpu.roll(p, POOL_EXT - k, 0)
            k *= 2
        first = POOL_HALO - w // 2
        s = (pltpu.roll(p, POOL_EXT - first, 0) if first else p)[:POOL_TM]
        lo_i = jnp.maximum(pos - w // 2, 0)
        hi_i = jnp.minimum(pos + w // 2 - 1, SEQ - 1)
        cnt = (hi_i - lo_i + 1).astype(f32)
        d = (s / cnt - h[:, c0:c0 + POOL_GROUP]).astype(bf16)
        y = _dot(d, pw_ref[j]) * ps_ref[:, c0:c0 + POOL_GROUP]
        o_ref[:, c0:c0 + POOL_GROUP] = x[:, c0:c0 + POOL_GROUP] + gt_ref[:, c0:c0 + POOL_GROUP] * y


def _pool_mixer(x, mods, g, p_w, p_scale):
    tm = POOL_TM
    hb = tm // POOL_HALO
    n_halo_blocks = N_LAT // POOL_HALO
    return pl.pallas_call(
        _pool_kernel,
        out_shape=jax.ShapeDtypeStruct((N_LAT, D), f32),
        grid=(N_LAT // tm,),
        in_specs=[
            pl.BlockSpec((POOL_HALO, D), lambda t: (jnp.maximum(t * hb - 1, 0), 0)),
            pl.BlockSpec((tm, D), lambda t: (t, 0)),
            pl.BlockSpec((POOL_HALO, D), lambda t: (jnp.minimum((t + 1) * hb, n_halo_blocks - 1), 0)),
            _mod_spec(0, tm), _mod_spec(1, tm), _mod_spec(2, tm),
            _full_spec((1, D)),
            _full_spec((len(POOL_WINDOWS), POOL_GROUP, POOL_GROUP)),
            _full_spec((1, D)),
        ],
        out_specs=pl.BlockSpec((tm, D), lambda t: (t, 0)),
        compiler_params=_params(("parallel",)),
        name="pool_mixer",
    )(x, x, x, mods, mods, mods, g, p_w, p_scale)


ROUTER_TM = 512


def _pack_bf16_halves(a):
    bits = lax.bitcast_convert_type(a.astype(bf16).astype(f32), jnp.uint32)
    half = a.shape[1] // 2
    return (bits[:, half:] & jnp.uint32(0xFFFF0000)) | (bits[:, :half] >> 16)


def _unpack_bf16_halves(p):
    lo = lax.bitcast_convert_type(p << 16, f32)
    hi = lax.bitcast_convert_type(p & jnp.uint32(0xFFFF0000), f32)
    return jnp.concatenate([lo, hi], axis=1)


def _router_kernel(x_ref, sh_ref, sc_ref, g_ref, wr_ref, ltri_ref, hp_ref, route_ref, cnt_ref, run_ref):
    @pl.when(pl.program_id(0) == 0)
    def _():
        run_ref[...] = jnp.zeros_like(run_ref)

    h = _modulate(x_ref[...], g_ref[...], sh_ref[...], sc_ref[...])
    hp_ref[...] = _pack_bf16_halves(h)
    h_hi = h.astype(bf16)
    h_lo = (h - h_hi.astype(f32)).astype(bf16)
    p1 = _dot(h_hi, wr_ref[0])
    logits = p1 + pltpu.roll(p1, LANES - N_EXPERTS, 1) + _dot(h_lo, wr_ref[1])
    lane = lax.broadcasted_iota(jnp.int32, logits.shape, 1)
    neg = jnp.float32(-jnp.inf)
    logits = jnp.where(lane < N_EXPERTS, logits, neg)
    m1 = jnp.max(logits, axis=-1, keepdims=True)
    i1 = jnp.min(jnp.where(logits == m1, lane, LANES), axis=-1, keepdims=True)
    rest = jnp.where(lane == i1, neg, logits)
    m2 = jnp.max(rest, axis=-1, keepdims=True)
    i2 = jnp.min(jnp.where(rest == m2, lane, LANES), axis=-1, keepdims=True)
    e2 = jnp.exp(m2 - m1)
    g1 = 1.0 / (1.0 + e2)
    g2 = e2 / (1.0 + e2)
    oh1 = jnp.where(lane == i1, 1.0, 0.0)
    oh2 = jnp.where(lane == i2, 1.0, 0.0)
    ltri = ltri_ref[...]
    before1 = _dot(ltri, oh1.astype(bf16))
    before2 = _dot(ltri, oh2.astype(bf16))
    tot1 = jnp.sum(oh1, axis=0, keepdims=True)
    tot2 = jnp.sum(oh2, axis=0, keepdims=True)
    run = run_ref[...]
    lp1 = jnp.sum(oh1 * (run + before1), axis=-1, keepdims=True)
    lp2 = jnp.sum(oh2 * (run + tot1 + before2), axis=-1, keepdims=True)
    run = run + tot1 + tot2
    run_ref[...] = run
    cnt_ref[...] = run
    route = jnp.where(lane == 0, i1.astype(f32),
                      jnp.where(lane == 1, i2.astype(f32),
                                jnp.where(lane == 2, g1,
                                          jnp.where(lane == 3, g2,
                                                    jnp.where(lane == 4, lp1, jnp.where(lane == 5, lp2, 0.0))))))
    route_ref[...] = route


def _router(x, mods, g, wr_pad):
    tm = ROUTER_TM
    ltri = jnp.asarray(np.tril(np.ones((tm, tm), np.float32), -1), dtype=bf16)
    return pl.pallas_call(
        _router_kernel,
        out_shape=(jax.ShapeDtypeStruct((N_LAT, D // 2), jnp.uint32),
                   jax.ShapeDtypeStruct((N_LAT, LANES), f32),
                   jax.ShapeDtypeStruct((1, LANES), f32)),
        grid=(N_LAT // tm,),
        in_specs=[
            pl.BlockSpec((tm, D), lambda t: (t, 0)),
            _mod_spec(3, tm), _mod_spec(4, tm),
            _full_spec((1, D)),
            _full_spec((2, D, LANES)),
            _full_spec((tm, tm)),
        ],
        out_specs=(pl.BlockSpec((tm, D // 2), lambda t: (t, 0)),
                   pl.BlockSpec((tm, LANES), lambda t: (t, 0)),
                   _full_spec((1, LANES))),
        scratch_shapes=[pltpu.VMEM((1, LANES), f32)],
        compiler_params=_params(("arbitrary",)),
        name="router",
    )(x, mods, mods, g, wr_pad, ltri)


MOE_TM = 512
MOE_TF = 1792
MOE_TILES = (2 * N_LAT) // MOE_TM + N_EXPERTS
MOE_ROWS = MOE_TILES * MOE_TM


def _expert_kernel(te_ref, nt_ref, xs_ref, wg_ref, wu_ref, wd_ref, y_ref, xb_ref, acc_ref):
    t = pl.program_id(0)
    j = pl.program_id(1)
    last = pl.num_programs(1) - 1
    active = t < nt_ref[0]

    @pl.when(active)
    def _():
        @pl.when(j == 0)
        def _():
            xb_ref[...] = _unpack_bf16_halves(xs_ref[...]).astype(bf16)
            acc_ref[...] = jnp.zeros_like(acc_ref)

        xs = xb_ref[...]
        a = _dot(xs, wg_ref[...])
        b = _dot(xs, wu_ref[...])
        act = (a * jax.nn.sigmoid(a) * b).astype(bf16)
        acc_ref[...] += _dot(act, wd_ref[...])

        @pl.when(j == last)
        def _():
            y_ref[...] = _pack_bf16_halves(acc_ref[...])

    @pl.when(jnp.logical_not(active) & (j == last))
    def _():
        y_ref[...] = jnp.zeros_like(y_ref)


def _experts(tile_expert, num_tiles, xs, wg, wu, wd):
    tm, tf = MOE_TM, MOE_TF
    nj = EXPERT_DIM // tf

    def w_col(t, j, te, nt):
        jj = jnp.where(t < nt[0], j, nj - 1)
        return (te[t], 0, jj)

    def w_row(t, j, te, nt):
        jj = jnp.where(t < nt[0], j, nj - 1)
        return (te[t], jj, 0)

    return pl.pallas_call(
        _expert_kernel,
        out_shape=jax.ShapeDtypeStruct((MOE_ROWS, D // 2), jnp.uint32),
        grid_spec=pltpu.PrefetchScalarGridSpec(
            num_scalar_prefetch=2,
            grid=(MOE_TILES, nj),
            in_specs=[
                pl.BlockSpec((tm, D // 2), lambda t, j, te, nt: (jnp.minimum(t, jnp.maximum(nt[0] - 1, 0)), 0)),
                pl.BlockSpec((None, D, tf), w_col),
                pl.BlockSpec((None, D, tf), w_col),
                pl.BlockSpec((None, tf, D), w_row),
            ],
            out_specs=pl.BlockSpec((tm, D // 2), lambda t, j, te, nt: (t, 0)),
            scratch_shapes=[pltpu.VMEM((tm, D), bf16), pltpu.VMEM((tm, D), f32)],
        ),
        compiler_params=_params(("arbitrary", "arbitrary")),
        name="moe_experts",
    )(tile_expert, num_tiles, xs, wg, wu, wd)


COMBINE_TM = 512


def _combine_kernel(x_ref, y1_ref, y2_ref, route_ref, gt_ref, o_ref):
    route = route_ref[...]
    mix = route[:, 2:3] * _unpack_bf16_halves(y1_ref[...]) + route[:, 3:4] * _unpack_bf16_halves(y2_ref[...])
    o_ref[...] = x_ref[...] + gt_ref[...] * mix


def _combine(x, y1, y2, route, mods):
    tm = COMBINE_TM
    return pl.pallas_call(
        _combine_kernel,
        out_shape=jax.ShapeDtypeStruct((N_LAT, D), f32),
        grid=(N_LAT // tm,),
        in_specs=[
            pl.BlockSpec((tm, D), lambda t: (t, 0)),
            pl.BlockSpec((tm, D // 2), lambda t: (t, 0)),
            pl.BlockSpec((tm, D // 2), lambda t: (t, 0)),
            pl.BlockSpec((tm, LANES), lambda t: (t, 0)),
            _mod_spec(5, tm),
        ],
        out_specs=pl.BlockSpec((tm, D), lambda t: (t, 0)),
        compiler_params=_params(("parallel",)),
        name="moe_combine",
    )(x, y1, y2, route, mods)


def _expert_table(table, e):
    ids = jnp.arange(N_EXPERTS, dtype=jnp.int32)
    return jnp.sum(jnp.where(e[:, None] == ids[None, :], table[None, :], 0), axis=1)


def _take_rows(a, idx):
    return a.at[idx].get(mode="promise_in_bounds")


def _moe(x, mods, g, wr_pad, wg, wu, wd, expert_base):
    tm = MOE_TM
    hp, route, cnt = _router(x, mods, g, wr_pad)
    e1 = route[:, 0].astype(jnp.int32)
    e2 = route[:, 1].astype(jnp.int32)
    counts = cnt[0, :N_EXPERTS].astype(jnp.int32)
    tiles_per_e = (counts + tm - 1) // tm
    tile_end = jnp.cumsum(tiles_per_e)
    group_base = (tile_end - tiles_per_e) * tm
    dense_base = jnp.cumsum(counts) - counts
    pos1 = _expert_table(group_base, e1) + route[:, 4].astype(jnp.int32)
    pos2 = _expert_table(group_base, e2) + route[:, 5].astype(jnp.int32)
    num_tiles = tile_end[-1]
    tile_ids = jnp.arange(MOE_TILES, dtype=jnp.int32)
    tile_expert = jnp.sum(tile_end[None, :] <= jnp.minimum(tile_ids, num_tiles - 1)[:, None], axis=1)
    tile_expert = jnp.minimum(tile_expert, N_EXPERTS - 1).astype(jnp.int32)
    tok = jnp.arange(N_LAT, dtype=jnp.int32)
    _, sorted_tok = lax.sort_key_val(jnp.concatenate([pos1, pos2]), jnp.concatenate([tok, tok]))
    row_expert = jnp.repeat(tile_expert, tm)
    src = jnp.arange(MOE_ROWS, dtype=jnp.int32) - _expert_table(group_base - dense_base, row_expert)
    row_token = _take_rows(sorted_tok, jnp.clip(src, 0, 2 * N_LAT - 1))
    xs = _take_rows(hp, row_token)
    y = _experts(tile_expert + expert_base, num_tiles.reshape(1).astype(jnp.int32), xs, wg, wu, wd)
    return _combine(x, _take_rows(y, pos1), _take_rows(y, pos2), route, mods)


def _rope_tables():
    rows = SEQ // GRID_W
    r = jnp.repeat(jnp.arange(rows), GRID_W)
    col = jnp.tile(jnp.arange(GRID_W), rows)
    inv = ROPE_THETA ** (-jnp.arange(ROPE_FREQS, dtype=f32) / ROPE_FREQS)
    ang = jnp.stack([r, col], axis=-1).astype(f32)[..., None] * inv
    c = jnp.cos(ang).reshape(SEQ, 2 * ROPE_FREQS)
    s = jnp.sin(ang).reshape(SEQ, 2 * ROPE_FREQS)
    cos_h = jnp.concatenate([c, c], axis=-1)
    sin_h = jnp.concatenate([-s, s], axis=-1)
    cos_t = jnp.tile(cos_h, (1, HEADS_PER_VREG))
    sin_t = jnp.tile(sin_h, (1, HEADS_PER_VREG))
    cos_t = jnp.concatenate([cos_t, jnp.ones((QKV_TM, LANES), f32)], axis=0)
    sin_t = jnp.concatenate([sin_t, jnp.zeros((QKV_TM, LANES), f32)], axis=0)
    return cos_t, sin_t


def kernel(x, c, ctx, c_ctx, ada_w, ada_b, norm_g, a_w_in, a_v_g, a_ws, a_bs, a_w_out, b_w_qkv, b_q_g, b_k_g,
           b_w_o, p_w, p_scale, f_w_gate, f_w_up, f_w_down, m_router, m_w_gate, m_w_up, m_w_down):
    cond = jnp.concatenate([c, c_ctx[None, :], jnp.zeros((MOD_ROWS - BATCH - 1, D), f32)], axis=0)
    ada = _ada_all(cond, ada_w, ada_b)
    mods = ada.reshape(DEPTH, MOD_ROWS, 6, 1, D).transpose(0, 2, 1, 3, 4)

    def gmlp_weights(j):
        b_full = jnp.repeat(a_bs[j].T, A_GW, axis=1)
        return (a_w_in[j].astype(bf16), a_v_g[j][None, :], a_ws[j].astype(bf16), b_full,
                a_w_out[j].astype(bf16))

    n_moe = m_w_gate.shape[0]
    moe_wg = m_w_gate.astype(bf16).reshape(n_moe * N_EXPERTS, D, EXPERT_DIM)
    moe_wu = m_w_up.astype(bf16).reshape(n_moe * N_EXPERTS, D, EXPERT_DIM)
    moe_wd = m_w_down.astype(bf16).reshape(n_moe * N_EXPERTS, EXPERT_DIM, D)

    def moe_weights(f):
        w_hi = m_router[f].astype(bf16)
        w_lo = (m_router[f] - w_hi.astype(f32)).astype(bf16)
        zeros = jnp.zeros((D, LANES - 2 * N_EXPERTS), bf16)
        wr = jnp.stack([jnp.concatenate([w_hi, w_lo, zeros], axis=1),
                        jnp.concatenate([w_hi, jnp.zeros_like(w_lo), zeros], axis=1)])
        return wr, moe_wg, moe_wu, moe_wd, f * N_EXPERTS

    x_all = jnp.concatenate([x.reshape(N_LAT, D), ctx.reshape(N_CTX, D)], axis=0)

    x_all = _gmlp_mixer(x_all, mods[0], norm_g[0, 0][None, :], *gmlp_weights(0))
    ffn_w = (f_w_gate.astype(bf16), f_w_up.astype(bf16), f_w_down.astype(bf16))
    x_all = _dense_ffn(x_all, mods[0], norm_g[0, 1][None, :], *ffn_w, 0)

    gain = jnp.concatenate([jnp.tile(b_q_g[0], N_HEADS), jnp.tile(b_k_g[0], N_KV)])[None, :]
    head_of_lane = np.arange(LANES) // HEAD_DIM
    bd = jnp.asarray((head_of_lane[:, None] == head_of_lane[None, :]).astype(np.float32), dtype=bf16)
    cos_t, sin_t = _rope_tables()
    qkv = _qkv_proj(x_all, mods[1], norm_g[1, 0][None, :], b_w_qkv[0].astype(bf16), gain, bd, cos_t, sin_t)

    def keys_first(a):
        return jnp.concatenate([a[N_LAT:].reshape(BATCH, CTX_LEN, -1), a[:N_LAT].reshape(BATCH, SEQ, -1)], axis=1)

    k = keys_first(qkv[:, N_HEADS * HEAD_DIM:QK_COLS]).reshape(BATCH, LK, N_KV, HEAD_DIM)
    v = keys_first(qkv[:, QK_COLS:]).reshape(BATCH, LK, N_KV, HEAD_DIM)
    kh = k.transpose(0, 2, 1, 3)
    kk = jnp.concatenate([kh] * HEADS_PER_VREG, axis=3)
    vt = v.transpose(0, 2, 3, 1)
    vt = jnp.concatenate([vt, jnp.ones((BATCH, N_KV, LANES - HEAD_DIM, LK), bf16)], axis=2)
    qt = qkv[:N_LAT, :N_HEADS * HEAD_DIM].T
    o = _attention(qt, kk, vt).T
    xl = _attn_out(o, x_all, mods[1], b_w_o[0].astype(bf16))
    xl = _moe(xl, mods[1], norm_g[1, 1][None, :], *moe_weights(0))

    xl = _pool_mixer(xl, mods[2], norm_g[2, 0][None, :], p_w[0].astype(bf16), p_scale[0][None, :])
    xl = _dense_ffn(xl, mods[2], norm_g[2, 1][None, :], *ffn_w, 1)

    xl = _gmlp_mixer(xl, mods[3], norm_g[3, 0][None, :], *gmlp_weights(1))
    xl = _moe(xl, mods[3], norm_g[3, 1][None, :], *moe_weights(1))
    return xl.reshape(BATCH, SEQ, D)
```

```python
import functools

import jax
import jax.numpy as jnp
import numpy as np
from jax import lax
from jax.experimental import pallas as pl
from jax.experimental.pallas import tpu as pltpu

D = 1024
BATCH = 4
SEQ = 4096
DEPTH = 4
GRID_W = 64
CTX_LEN = 256
CHUNK = 128
A_DIM = 2 * D
A_GROUPS = 8
A_GW = A_DIM // A_GROUPS
N_HEADS = 16
N_KV = 4
HEAD_DIM = 64
ROPE_FREQS = HEAD_DIM // 4
ROPE_THETA = 10000.0
POOL_WINDOWS = (2, 4, 8, 16)
POOL_GROUP = D // 4
FFN_DIM = 2816
N_EXPERTS = 8
EXPERT_DIM = 3584
EPS = 1e-6

N_LAT = BATCH * SEQ
N_CTX = BATCH * CTX_LEN
N_ALL = N_LAT + N_CTX
LK = CTX_LEN + SEQ
MOD_ROWS = 8
LANES = 128
QK_COLS = (N_HEADS + N_KV) * HEAD_DIM
QKV_COLS = (N_HEADS + 2 * N_KV) * HEAD_DIM

VMEM_LIMIT = 56 * 1024 * 1024

bf16 = jnp.bfloat16
f32 = jnp.float32


def _params(sem, vmem=VMEM_LIMIT):
    return pltpu.CompilerParams(dimension_semantics=sem, vmem_limit_bytes=vmem)


def _mod_row(t, tm):
    start = t * tm
    return jnp.where(start < N_LAT, start // SEQ, BATCH)


def _mod_spec(chunk, tm):
    return pl.BlockSpec((None, None, 1, D), lambda t, *_: (chunk, _mod_row(t, tm), 0, 0))


def _full_spec(shape):
    nd = len(shape)
    return pl.BlockSpec(shape, lambda *_: (0,) * nd)


def _modulate(x, g, shift, scale):
    ms = jnp.mean(x * x, axis=-1, keepdims=True)
    return (x * lax.rsqrt(ms + EPS) * g) * (1.0 + scale) + shift


LOG2E = 1.4426950408889634


def _gelu_tanh(z):
    k0 = -2.0 * 0.7978845608028654 * LOG2E
    return z / (1.0 + jnp.exp2(z * (z * z * (k0 * 0.044715) + k0)))


def _dot(a, b):
    return jnp.dot(a, b, preferred_element_type=f32)


ADA_TN = 1536


def _ada_kernel(cond_ref, w_ref, b_ref, o_ref):
    cond = cond_ref[...]
    s = (cond * jax.nn.sigmoid(cond)).astype(bf16)
    o_ref[...] = _dot(s, w_ref[...].astype(bf16)) + b_ref[...]


def _ada_all(cond, ada_w, ada_b):
    return pl.pallas_call(
        _ada_kernel,
        out_shape=jax.ShapeDtypeStruct((DEPTH, MOD_ROWS, 6 * D), f32),
        grid=(DEPTH, 6 * D // ADA_TN),
        in_specs=[
            _full_spec((MOD_ROWS, D)),
            pl.BlockSpec((None, D, ADA_TN), lambda i, n: (i, 0, n)),
            pl.BlockSpec((None, 1, ADA_TN), lambda i, n: (i, 0, n)),
        ],
        out_specs=pl.BlockSpec((None, MOD_ROWS, ADA_TN), lambda i, n: (i, 0, n)),
        compiler_params=_params(("parallel", "parallel")),
        name="ada",
    )(cond, ada_w, ada_b.reshape(DEPTH, 1, 6 * D))


GMLP_TM = 512


def _gmlp_kernel(x_ref, sh_ref, sc_ref, gt_ref, g_ref, win_ref, vg_ref, ws_ref, bs_ref, wout_ref,
                 o_ref, uv_ref):
    x = x_ref[...]
    h = _modulate(x, g_ref[...], sh_ref[...], sc_ref[...]).astype(bf16)
    z = _gelu_tanh(_dot(h, win_ref[...]))
    u = z[:, :A_DIM]
    v = z[:, A_DIM:]
    v = (v * lax.rsqrt(jnp.mean(v * v, axis=-1, keepdims=True) + EPS) * vg_ref[...]).astype(bf16)
    for c in range(GMLP_TM // CHUNK):
        r0 = c * CHUNK
        for g in range(A_GROUPS):
            c0 = g * A_GW
            mixed = _dot(ws_ref[g], v[r0:r0 + CHUNK, c0:c0 + A_GW]) + bs_ref[:, c0:c0 + A_GW]
            uv_ref[r0:r0 + CHUNK, c0:c0 + A_GW] = (u[r0:r0 + CHUNK, c0:c0 + A_GW] * mixed).astype(bf16)
    y = _dot(uv_ref[...], wout_ref[...])
    o_ref[...] = x + gt_ref[...] * y


def _gmlp_mixer(x, mods, g, w_in, v_g, w_s, b_full, w_out, layer):
    rows = x.shape[0]
    tm = GMLP_TM
    return pl.pallas_call(
        _gmlp_kernel,
        out_shape=jax.ShapeDtypeStruct((rows, D), f32),
        grid=(rows // tm,),
        in_specs=[
            pl.BlockSpec((tm, D), lambda t: (t, 0)),
            _mod_spec(0, tm), _mod_spec(1, tm), _mod_spec(2, tm),
            _full_spec((1, D)),
            _resident_spec((D, 2 * A_DIM), layer),
            _full_spec((1, A_DIM)),
            _full_spec((A_GROUPS, CHUNK, CHUNK)),
            _full_spec((CHUNK, A_DIM)),
            _resident_spec((A_DIM, D), layer),
        ],
        out_specs=pl.BlockSpec((tm, D), lambda t: (t, 0)),
        scratch_shapes=[pltpu.VMEM((tm, A_DIM), bf16)],
        compiler_params=_params(("parallel",)),
        name="gmlp_mixer",
    )(x, mods, mods, mods, g, w_in, v_g, w_s, b_full, w_out)


FFN_TM = 512


def _ffn_kernel(x_ref, sh_ref, sc_ref, gt_ref, g_ref, wg_ref, wu_ref, wd_ref, o_ref):
    x = x_ref[...]
    h = _modulate(x, g_ref[...], sh_ref[...], sc_ref[...]).astype(bf16)
    a = _dot(h, wg_ref[...])
    b = _dot(h, wu_ref[...])
    act = (a * jax.nn.sigmoid(a) * b).astype(bf16)
    o_ref[...] = x + gt_ref[...] * _dot(act, wd_ref[...])


def _resident_spec(shape, layer):
    nd = len(shape)
    return pl.BlockSpec((None,) + shape, lambda *_: (layer,) + (0,) * nd, pipeline_mode=pl.Buffered(1))


def _dense_ffn(x, mods, g, wg, wu, wd, layer):
    rows = x.shape[0]
    tm = FFN_TM
    return pl.pallas_call(
        _ffn_kernel,
        out_shape=jax.ShapeDtypeStruct((rows, D), f32),
        grid=(rows // tm,),
        in_specs=[
            pl.BlockSpec((tm, D), lambda t: (t, 0)),
            _mod_spec(3, tm), _mod_spec(4, tm), _mod_spec(5, tm),
            _full_spec((1, D)),
            _resident_spec((D, FFN_DIM), layer),
            _resident_spec((D, FFN_DIM), layer),
            _resident_spec((FFN_DIM, D), layer),
        ],
        out_specs=pl.BlockSpec((tm, D), lambda t: (t, 0)),
        compiler_params=_params(("parallel",)),
        name="dense_ffn",
    )(x, mods, mods, mods, g, wg, wu, wd)


QKV_TM = 512
HEADS_PER_VREG = LANES // HEAD_DIM
Q_SCALE = HEAD_DIM ** -0.5 * LOG2E


def _qkv_kernel(x_ref, sh_ref, sc_ref, g_ref, w_ref, gain_ref, bd_ref, cos_ref, sin_ref, o_ref):
    h = _modulate(x_ref[...], g_ref[...], sh_ref[...], sc_ref[...]).astype(bf16)
    qkv = _dot(h, w_ref[...])
    lane = lax.broadcasted_iota(jnp.int32, (QKV_TM, LANES), 1)
    first_half = (lane % HEAD_DIM) < (HEAD_DIM // 2)
    cos = cos_ref[...]
    sin = sin_ref[...]
    bd = bd_ref[...]
    for cb in range(QKV_COLS // LANES):
        c0 = cb * LANES
        blk = qkv[:, c0:c0 + LANES]
        if c0 < QK_COLS:
            sq = blk * blk
            hi = sq.astype(bf16)
            lo = (sq - hi.astype(f32)).astype(bf16)
            ms = (_dot(hi, bd) + _dot(lo, bd)) * (1.0 / HEAD_DIM)
            y = blk * lax.rsqrt(ms + EPS) * gain_ref[:, c0:c0 + LANES]
            partner = jnp.where(first_half, pltpu.roll(y, LANES - HEAD_DIM // 2, 1),
                                pltpu.roll(y, HEAD_DIM // 2, 1))
            y = y * cos + partner * sin
            if c0 < N_HEADS * HEAD_DIM:
                y = y * Q_SCALE
            blk = y
        o_ref[:, c0:c0 + LANES] = blk.astype(bf16)


def _qkv_proj(x_all, mods, g, w_qkv, gain, bd, cos_t, sin_t):
    tm = QKV_TM
    tiles_per_seq = SEQ // tm

    def tab_idx(t):
        return (jnp.where(t * tm < N_LAT, t % tiles_per_seq, tiles_per_seq), 0)

    return pl.pallas_call(
        _qkv_kernel,
        out_shape=jax.ShapeDtypeStruct((N_ALL, QKV_COLS), bf16),
        grid=(N_ALL // tm,),
        in_specs=[
            pl.BlockSpec((tm, D), lambda t: (t, 0)),
            _mod_spec(0, tm), _mod_spec(1, tm),
            _full_spec((1, D)),
            _full_spec((D, QKV_COLS)),
            _full_spec((1, QK_COLS)),
            _full_spec((LANES, LANES)),
            pl.BlockSpec((tm, LANES), tab_idx),
            pl.BlockSpec((tm, LANES), tab_idx),
        ],
        out_specs=pl.BlockSpec((tm, QKV_COLS), lambda t: (t, 0)),
        compiler_params=_params(("parallel",)),
        name="qkv_proj",
    )(x_all, mods, mods, g, w_qkv, gain, bd, cos_t, sin_t)


ATT_TQ = 512
REP = N_HEADS // N_KV
ATT_GROUPS = 2
ATT_HEADS = ATT_GROUPS * REP
ATT_KC = 256
ATT_LAG = 2


def _attn_kernel(qt_ref, kk_ref, vt_ref, o_ref, *st_refs):
    row = lax.broadcasted_iota(jnp.int32, (LANES, ATT_TQ), 0)
    low = row < HEAD_DIM
    zero = jnp.zeros((LANES, ATT_TQ), bf16)
    n_chunks = LK // ATT_KC
    sub = 8
    n_buf = len(st_refs)

    def masked_q(h):
        q2 = qt_ref[(h // HEADS_PER_VREG) * LANES:(h // HEADS_PER_VREG + 1) * LANES, :]
        return jnp.where(low if h % HEADS_PER_VREG == 0 else ~low, q2, zero)

    col_max = {}
    for ph in range(ATT_HEADS + ATT_LAG):
        ha, hb = ph, ph - ATT_LAG
        qm = masked_q(ha) if ha < ATT_HEADS else None
        mx = None
        acc = None
        for c in range(n_chunks):
            rows = slice(c * ATT_KC, (c + 1) * ATT_KC)
            if ha < ATT_HEADS:
                st = _dot(kk_ref[ha // REP, rows, :], qm)
                st_refs[ha % n_buf][rows, :] = st
                cm = jnp.max(st.reshape(ATT_KC // sub, sub, ATT_TQ), axis=0)
                mx = cm if mx is None else jnp.maximum(mx, cm)
            if hb >= 0:
                pt = jnp.exp2(st_refs[hb % n_buf][rows, :] - col_max[hb]).astype(bf16)
                part = _dot(vt_ref[hb // REP, :, rows], pt)
                acc = part if acc is None else acc + part
        if hb >= 0:
            o_ref[hb * HEAD_DIM:(hb + 1) * HEAD_DIM, :] = (
                acc[:HEAD_DIM] / acc[HEAD_DIM:HEAD_DIM + 1]).astype(bf16)
        if ha < ATT_HEADS:
            col_max[ha] = jnp.max(mx, axis=0, keepdims=True)


def _attention(qt, kk, vt):
    tq = ATT_TQ
    qt_per_seq = SEQ // tq
    gw = ATT_HEADS * HEAD_DIM
    return pl.pallas_call(
        _attn_kernel,
        out_shape=jax.ShapeDtypeStruct((D, N_LAT), bf16),
        grid=(BATCH, N_KV // ATT_GROUPS, qt_per_seq),
        in_specs=[
            pl.BlockSpec((gw, tq), lambda b, g, t: (g, b * qt_per_seq + t)),
            pl.BlockSpec((None, ATT_GROUPS, LK, LANES), lambda b, g, t: (b, g, 0, 0)),
            pl.BlockSpec((None, ATT_GROUPS, LANES, LK), lambda b, g, t: (b, g, 0, 0)),
        ],
        out_specs=pl.BlockSpec((gw, tq), lambda b, g, t: (g, b * qt_per_seq + t)),
        scratch_shapes=[pltpu.VMEM((LK, tq), f32)] * (ATT_LAG + 1),
        compiler_params=_params(("parallel", "parallel", "arbitrary")),
        name="attention",
    )(qt, kk, vt)


PROJ_TM = 512


def _oproj_kernel(o_ref, x_ref, gt_ref, w_ref, out_ref):
    out_ref[...] = x_ref[...] + gt_ref[...] * _dot(o_ref[...], w_ref[...])


def _attn_out(o, x_all, mods, w_o):
    tm = PROJ_TM
    return pl.pallas_call(
        _oproj_kernel,
        out_shape=jax.ShapeDtypeStruct((N_LAT, D), f32),
        grid=(N_LAT // tm,),
        in_specs=[
            pl.BlockSpec((tm, D), lambda t: (t, 0)),
            pl.BlockSpec((tm, D), lambda t: (t, 0)),
            _mod_spec(2, tm),
            _full_spec((D, D)),
        ],
        out_specs=pl.BlockSpec((tm, D), lambda t: (t, 0)),
        compiler_params=_params(("parallel",)),
        name="attn_out",
    )(o, x_all, mods, w_o)


POOL_TM = 512
POOL_HALO = 8
POOL_EXT = POOL_TM + 2 * POOL_HALO
assert max(POOL_WINDOWS) // 2 <= POOL_HALO


def _pool_kernel(xp_ref, x_ref, xn_ref, sh_ref, sc_ref, gt_ref, g_ref, pw_ref, ps_ref, o_ref):
    t = pl.program_id(0)
    tiles_per_seq = SEQ // POOL_TM
    ts = t % tiles_per_seq
    x = x_ref[...]
    g, sh, sc = g_ref[...], sh_ref[...], sc_ref[...]
    h = _modulate(x, g, sh, sc)
    hp = jnp.where(ts > 0, _modulate(xp_ref[...], g, sh, sc), 0.0)
    hn = jnp.where(ts < tiles_per_seq - 1, _modulate(xn_ref[...], g, sh, sc), 0.0)
    ext = jnp.concatenate([hp, h, hn], axis=0)
    pos = ts * POOL_TM + lax.broadcasted_iota(jnp.int32, (POOL_TM, 1), 0)
    for j, w in enumerate(POOL_WINDOWS):
        c0 = j * POOL_GROUP
        p = ext[:, c0:c0 + POOL_GROUP]
        k = 1
        while k < w:
            p = p + pltpu.roll(p, POOL_EXT - k, 0)
            k *= 2
        first = POOL_HALO - w // 2
        s = (pltpu.roll(p, POOL_EXT - first, 0) if first else p)[:POOL_TM]
        lo_i = jnp.maximum(pos - w // 2, 0)
        hi_i = jnp.minimum(pos + w // 2 - 1, SEQ - 1)
        cnt = (hi_i - lo_i + 1).astype(f32)
        d = (s / cnt - h[:, c0:c0 + POOL_GROUP]).astype(bf16)
        y = _dot(d, pw_ref[j]) * ps_ref[:, c0:c0 + POOL_GROUP]
        o_ref[:, c0:c0 + POOL_GROUP] = x[:, c0:c0 + POOL_GROUP] + gt_ref[:, c0:c0 + POOL_GROUP] * y


def _pool_mixer(x, mods, g, p_w, p_scale):
    tm = POOL_TM
    hb = tm // POOL_HALO
    n_halo_blocks = N_LAT // POOL_HALO
    return pl.pallas_call(
        _pool_kernel,
        out_shape=jax.ShapeDtypeStruct((N_LAT, D), f32),
        grid=(N_LAT // tm,),
        in_specs=[
            pl.BlockSpec((POOL_HALO, D), lambda t: (jnp.maximum(t * hb - 1, 0), 0)),
            pl.BlockSpec((tm, D), lambda t: (t, 0)),
            pl.BlockSpec((POOL_HALO, D), lambda t: (jnp.minimum((t + 1) * hb, n_halo_blocks - 1), 0)),
            _mod_spec(0, tm), _mod_spec(1, tm), _mod_spec(2, tm),
            _full_spec((1, D)),
            _full_spec((len(POOL_WINDOWS), POOL_GROUP, POOL_GROUP)),
            _full_spec((1, D)),
        ],
        out_specs=pl.BlockSpec((tm, D), lambda t: (t, 0)),
        compiler_params=_params(("parallel",)),
        name="pool_mixer",
    )(x, x, x, mods, mods, mods, g, p_w, p_scale)


ROUTER_TM = 512


def _pack_bf16_halves(a):
    bits = lax.bitcast_convert_type(a.astype(bf16).astype(f32), jnp.uint32)
    half = a.shape[1] // 2
    return (bits[:, half:] & jnp.uint32(0xFFFF0000)) | (bits[:, :half] >> 16)


def _unpack_bf16_halves(p):
    lo = lax.bitcast_convert_type(p << 16, f32)
    hi = lax.bitcast_convert_type(p & jnp.uint32(0xFFFF0000), f32)
    return jnp.concatenate([lo, hi], axis=1)


def _router_kernel(x_ref, sh_ref, sc_ref, g_ref, wr_ref, ltri_ref, hp_ref, route_ref, cnt_ref, run_ref):
    @pl.when(pl.program_id(0) == 0)
    def _():
        run_ref[...] = jnp.zeros_like(run_ref)

    h = _modulate(x_ref[...], g_ref[...], sh_ref[...], sc_ref[...])
    hp_ref[...] = _pack_bf16_halves(h)
    h_hi = h.astype(bf16)
    h_lo = (h - h_hi.astype(f32)).astype(bf16)
    p1 = _dot(h_hi, wr_ref[0])
    logits = p1 + pltpu.roll(p1, LANES - N_EXPERTS, 1) + _dot(h_lo, wr_ref[1])
    lane = lax.broadcasted_iota(jnp.int32, logits.shape, 1)
    neg = jnp.float32(-jnp.inf)
    logits = jnp.where(lane < N_EXPERTS, logits, neg)
    m1 = jnp.max(logits, axis=-1, keepdims=True)
    i1 = jnp.min(jnp.where(logits == m1, lane, LANES), axis=-1, keepdims=True)
    rest = jnp.where(lane == i1, neg, logits)
    m2 = jnp.max(rest, axis=-1, keepdims=True)
    i2 = jnp.min(jnp.where(rest == m2, lane, LANES), axis=-1, keepdims=True)
    e2 = jnp.exp(m2 - m1)
    g1 = 1.0 / (1.0 + e2)
    g2 = e2 / (1.0 + e2)
    oh1 = jnp.where(lane == i1, 1.0, 0.0)
    oh2 = jnp.where(lane == i2, 1.0, 0.0)
    ltri = ltri_ref[...]
    before1 = _dot(ltri, oh1.astype(bf16))
    before2 = _dot(ltri, oh2.astype(bf16))
    tot1 = jnp.sum(oh1, axis=0, keepdims=True)
    tot2 = jnp.sum(oh2, axis=0, keepdims=True)
    run = run_ref[...]
    lp1 = jnp.sum(oh1 * (run + before1), axis=-1, keepdims=True)
    lp2 = jnp.sum(oh2 * (run + tot1 + before2), axis=-1, keepdims=True)
    run = run + tot1 + tot2
    run_ref[...] = run
    cnt_ref[...] = run
    route = jnp.where(lane == 0, i1.astype(f32),
                      jnp.where(lane == 1, i2.astype(f32),
                                jnp.where(lane == 2, g1,
                                          jnp.where(lane == 3, g2,
                                                    jnp.where(lane == 4, lp1, jnp.where(lane == 5, lp2, 0.0))))))
    route_ref[...] = route


def _router(x, mods, g, wr_pad):
    tm = ROUTER_TM
    ltri = jnp.asarray(np.tril(np.ones((tm, tm), np.float32), -1), dtype=bf16)
    return pl.pallas_call(
        _router_kernel,
        out_shape=(jax.ShapeDtypeStruct((N_LAT, D // 2), jnp.uint32),
                   jax.ShapeDtypeStruct((N_LAT, LANES), f32),
                   jax.ShapeDtypeStruct((1, LANES), f32)),
        grid=(N_LAT // tm,),
        in_specs=[
            pl.BlockSpec((tm, D), lambda t: (t, 0)),
            _mod_spec(3, tm), _mod_spec(4, tm),
            _full_spec((1, D)),
            _full_spec((2, D, LANES)),
            _full_spec((tm, tm)),
        ],
        out_specs=(pl.BlockSpec((tm, D // 2), lambda t: (t, 0)),
                   pl.BlockSpec((tm, LANES), lambda t: (t, 0)),
                   _full_spec((1, LANES))),
        scratch_shapes=[pltpu.VMEM((1, LANES), f32)],
        compiler_params=_params(("arbitrary",)),
        name="router",
    )(x, mods, mods, g, wr_pad, ltri)


MOE_TM = 512
MOE_TF = 1792
MOE_TILES = (2 * N_LAT) // MOE_TM + N_EXPERTS
MOE_ROWS = MOE_TILES * MOE_TM
MOE_XS_PARTS = 3
MOE_PART_TILES = MOE_TILES // MOE_XS_PARTS
assert MOE_PART_TILES * MOE_XS_PARTS == MOE_TILES


def _expert_kernel(te_ref, nt_ref, *refs):
    xs_refs = refs[:MOE_XS_PARTS]
    wg_ref, wu_ref, wd_ref, y_ref, xb_ref, acc_ref = refs[MOE_XS_PARTS:]
    t = pl.program_id(0)
    j = pl.program_id(1)
    last = pl.num_programs(1) - 1
    active = t < nt_ref[0]

    @pl.when(active)
    def _():
        @pl.when(j == 0)
        def _():
            for k, xs_ref in enumerate(xs_refs):
                @pl.when(t // MOE_PART_TILES == k)
                def _():
                    xb_ref[...] = _unpack_bf16_halves(xs_ref[...]).astype(bf16)
            acc_ref[...] = jnp.zeros_like(acc_ref)

        xs = xb_ref[...]
        a = _dot(xs, wg_ref[...])
        b = _dot(xs, wu_ref[...])
        act = (a * jax.nn.sigmoid(a) * b).astype(bf16)
        acc_ref[...] += _dot(act, wd_ref[...])

        @pl.when(j == last)
        def _():
            y_ref[...] = _pack_bf16_halves(acc_ref[...])

    @pl.when(jnp.logical_not(active) & (j == last))
    def _():
        y_ref[...] = jnp.zeros_like(y_ref)


def _experts(tile_expert, num_tiles, xs_parts, wg, wu, wd):
    tm, tf = MOE_TM, MOE_TF
    nj = EXPERT_DIM // tf

    def xs_spec(k):
        def index(t, j, te, nt):
            tt = jnp.minimum(t, jnp.maximum(nt[0] - 1, 0))
            return (jnp.clip(tt - k * MOE_PART_TILES, 0, MOE_PART_TILES - 1), 0)
        return pl.BlockSpec((tm, D // 2), index)

    def w_col(t, j, te, nt):
        jj = jnp.where(t < nt[0], j, nj - 1)
        return (te[t], 0, jj)

    def w_row(t, j, te, nt):
        jj = jnp.where(t < nt[0], j, nj - 1)
        return (te[t], jj, 0)

    return pl.pallas_call(
        _expert_kernel,
        out_shape=jax.ShapeDtypeStruct((MOE_ROWS, D // 2), jnp.uint32),
        grid_spec=pltpu.PrefetchScalarGridSpec(
            num_scalar_prefetch=2,
            grid=(MOE_TILES, nj),
            in_specs=[xs_spec(k) for k in range(MOE_XS_PARTS)] + [
                pl.BlockSpec((None, D, tf), w_col),
                pl.BlockSpec((None, D, tf), w_col),
                pl.BlockSpec((None, tf, D), w_row),
            ],
            out_specs=pl.BlockSpec((tm, D // 2), lambda t, j, te, nt: (t, 0)),
            scratch_shapes=[pltpu.VMEM((tm, D), bf16), pltpu.VMEM((tm, D), f32)],
        ),
        compiler_params=_params(("arbitrary", "arbitrary")),
        name="moe_experts",
    )(tile_expert, num_tiles, *xs_parts, wg, wu, wd)


COMBINE_TM = 512


def _combine_kernel(x_ref, y1_ref, y2_ref, route_ref, gt_ref, o_ref):
    route = route_ref[...]
    mix = route[:, 2:3] * _unpack_bf16_halves(y1_ref[...]) + route[:, 3:4] * _unpack_bf16_halves(y2_ref[...])
    o_ref[...] = x_ref[...] + gt_ref[...] * mix


def _combine(x, y1, y2, route, mods):
    tm = COMBINE_TM
    return pl.pallas_call(
        _combine_kernel,
        out_shape=jax.ShapeDtypeStruct((N_LAT, D), f32),
        grid=(N_LAT // tm,),
        in_specs=[
            pl.BlockSpec((tm, D), lambda t: (t, 0)),
            pl.BlockSpec((tm, D // 2), lambda t: (t, 0)),
            pl.BlockSpec((tm, D // 2), lambda t: (t, 0)),
            pl.BlockSpec((tm, LANES), lambda t: (t, 0)),
            _mod_spec(5, tm),
        ],
        out_specs=pl.BlockSpec((tm, D), lambda t: (t, 0)),
        compiler_params=_params(("parallel",)),
        name="moe_combine",
    )(x, y1, y2, route, mods)


def _expert_table(table, e):
    ids = jnp.arange(N_EXPERTS, dtype=jnp.int32)
    return jnp.sum(jnp.where(e[:, None] == ids[None, :], table[None, :], 0), axis=1)


def _take_rows(a, idx):
    return a.at[idx].get(mode="promise_in_bounds")


def _moe(x, mods, g, wr_pad, wg, wu, wd, expert_base):
    tm = MOE_TM
    hp, route, cnt = _router(x, mods, g, wr_pad)
    e1 = route[:, 0].astype(jnp.int32)
    e2 = route[:, 1].astype(jnp.int32)
    counts = cnt[0, :N_EXPERTS].astype(jnp.int32)
    tiles_per_e = (counts + tm - 1) // tm
    tile_end = jnp.cumsum(tiles_per_e)
    group_base = (tile_end - tiles_per_e) * tm
    dense_base = jnp.cumsum(counts) - counts
    pos1 = _expert_table(group_base, e1) + route[:, 4].astype(jnp.int32)
    pos2 = _expert_table(group_base, e2) + route[:, 5].astype(jnp.int32)
    num_tiles = tile_end[-1]
    tile_ids = jnp.arange(MOE_TILES, dtype=jnp.int32)
    tile_expert = jnp.sum(tile_end[None, :] <= jnp.minimum(tile_ids, num_tiles - 1)[:, None], axis=1)
    tile_expert = jnp.minimum(tile_expert, N_EXPERTS - 1).astype(jnp.int32)
    tok = jnp.arange(N_LAT, dtype=jnp.int32)
    _, sorted_tok = lax.sort_key_val(jnp.concatenate([pos1, pos2]), jnp.concatenate([tok, tok]))
    row_expert = jnp.repeat(tile_expert, tm)
    src = jnp.arange(MOE_ROWS, dtype=jnp.int32) - _expert_table(group_base - dense_base, row_expert)
    row_token = _take_rows(sorted_tok, jnp.clip(src, 0, 2 * N_LAT - 1))
    part_rows = MOE_PART_TILES * tm
    xs_parts = [_take_rows(hp, row_token[k * part_rows:(k + 1) * part_rows]) for k in range(MOE_XS_PARTS)]
    y = _experts(tile_expert + expert_base, num_tiles.reshape(1).astype(jnp.int32), xs_parts, wg, wu, wd)
    return _combine(x, _take_rows(y, pos1), _take_rows(y, pos2), route, mods)


def _rope_tables():
    rows = SEQ // GRID_W
    r = jnp.repeat(jnp.arange(rows), GRID_W)
    col = jnp.tile(jnp.arange(GRID_W), rows)
    inv = ROPE_THETA ** (-jnp.arange(ROPE_FREQS, dtype=f32) / ROPE_FREQS)
    ang = jnp.stack([r, col], axis=-1).astype(f32)[..., None] * inv
    c = jnp.cos(ang).reshape(SEQ, 2 * ROPE_FREQS)
    s = jnp.sin(ang).reshape(SEQ, 2 * ROPE_FREQS)
    cos_h = jnp.concatenate([c, c], axis=-1)
    sin_h = jnp.concatenate([-s, s], axis=-1)
    cos_t = jnp.tile(cos_h, (1, HEADS_PER_VREG))
    sin_t = jnp.tile(sin_h, (1, HEADS_PER_VREG))
    cos_t = jnp.concatenate([cos_t, jnp.ones((QKV_TM, LANES), f32)], axis=0)
    sin_t = jnp.concatenate([sin_t, jnp.zeros((QKV_TM, LANES), f32)], axis=0)
    return cos_t, sin_t


def kernel(x, c, ctx, c_ctx, ada_w, ada_b, norm_g, a_w_in, a_v_g, a_ws, a_bs, a_w_out, b_w_qkv, b_q_g, b_k_g,
           b_w_o, p_w, p_scale, f_w_gate, f_w_up, f_w_down, m_router, m_w_gate, m_w_up, m_w_down):
    cond = jnp.concatenate([c, c_ctx[None, :], jnp.zeros((MOD_ROWS - BATCH - 1, D), f32)], axis=0)
    ada = _ada_all(cond, ada_w, ada_b)
    mods = ada.reshape(DEPTH, MOD_ROWS, 6, 1, D).transpose(0, 2, 1, 3, 4)

    gmlp_w_in = a_w_in.astype(bf16)
    gmlp_w_out = a_w_out.astype(bf16)

    def gmlp_weights(j):
        b_full = jnp.repeat(a_bs[j].T, A_GW, axis=1)
        return gmlp_w_in, a_v_g[j][None, :], a_ws[j].astype(bf16), b_full, gmlp_w_out, j

    n_moe = m_w_gate.shape[0]
    moe_wg = m_w_gate.astype(bf16).reshape(n_moe * N_EXPERTS, D, EXPERT_DIM)
    moe_wu = m_w_up.astype(bf16).reshape(n_moe * N_EXPERTS, D, EXPERT_DIM)
    moe_wd = m_w_down.astype(bf16).reshape(n_moe * N_EXPERTS, EXPERT_DIM, D)

    def moe_weights(f):
        w_hi = m_router[f].astype(bf16)
        w_lo = (m_router[f] - w_hi.astype(f32)).astype(bf16)
        zeros = jnp.zeros((D, LANES - 2 * N_EXPERTS), bf16)
        wr = jnp.stack([jnp.concatenate([w_hi, w_lo, zeros], axis=1),
                        jnp.concatenate([w_hi, jnp.zeros_like(w_lo), zeros], axis=1)])
        return wr, moe_wg, moe_wu, moe_wd, f * N_EXPERTS

    x_all = jnp.concatenate([x.reshape(N_LAT, D), ctx.reshape(N_CTX, D)], axis=0)

    x_all = _gmlp_mixer(x_all, mods[0], norm_g[0, 0][None, :], *gmlp_weights(0))
    ffn_w = (f_w_gate.astype(bf16), f_w_up.astype(bf16), f_w_down.astype(bf16))
    x_all = _dense_ffn(x_all, mods[0], norm_g[0, 1][None, :], *ffn_w, 0)

    gain = jnp.concatenate([jnp.tile(b_q_g[0], N_HEADS), jnp.tile(b_k_g[0], N_KV)])[None, :]
    head_of_lane = np.arange(LANES) // HEAD_DIM
    bd = jnp.asarray((head_of_lane[:, None] == head_of_lane[None, :]).astype(np.float32), dtype=bf16)
    cos_t, sin_t = _rope_tables()
    qkv = _qkv_proj(x_all, mods[1], norm_g[1, 0][None, :], b_w_qkv[0].astype(bf16), gain, bd, cos_t, sin_t)

    def keys_first(a):
        return jnp.concatenate([a[N_LAT:].reshape(BATCH, CTX_LEN, -1), a[:N_LAT].reshape(BATCH, SEQ, -1)], axis=1)

    k = keys_first(qkv[:, N_HEADS * HEAD_DIM:QK_COLS]).reshape(BATCH, LK, N_KV, HEAD_DIM)
    v = keys_first(qkv[:, QK_COLS:]).reshape(BATCH, LK, N_KV, HEAD_DIM)
    kh = k.transpose(0, 2, 1, 3)
    kk = jnp.concatenate([kh] * HEADS_PER_VREG, axis=3)
    vt = v.transpose(0, 2, 3, 1)
    vt = jnp.concatenate([vt, jnp.ones((BATCH, N_KV, LANES - HEAD_DIM, LK), bf16)], axis=2)
    qt = qkv[:N_LAT, :N_HEADS * HEAD_DIM].T
    o = _attention(qt, kk, vt).T
    xl = _attn_out(o, x_all, mods[1], b_w_o[0].astype(bf16))
    xl = _moe(xl, mods[1], norm_g[1, 1][None, :], *moe_weights(0))

    xl = _pool_mixer(xl, mods[2], norm_g[2, 0][None, :], p_w[0].astype(bf16), p_scale[0][None, :])
    xl = _dense_ffn(xl, mods[2], norm_g[2, 1][None, :], *ffn_w, 1)

    xl = _gmlp_mixer(xl, mods[3], norm_g[3, 0][None, :], *gmlp_weights(1))
    xl = _moe(xl, mods[3], norm_g[3, 1][None, :], *moe_weights(1))
    return xl.reshape(BATCH, SEQ, D)
```

```python
import functools

import jax
import jax.numpy as jnp
import numpy as np
from jax import lax
from jax.experimental import pallas as pl
from jax.experimental.pallas import tpu as pltpu

D = 1024
BATCH = 4
SEQ = 4096
DEPTH = 4
GRID_W = 64
CTX_LEN = 256
CHUNK = 128
A_DIM = 2 * D
A_GROUPS = 8
A_GW = A_DIM // A_GROUPS
N_HEADS = 16
N_KV = 4
HEAD_DIM = 64
ROPE_FREQS = HEAD_DIM // 4
ROPE_THETA = 10000.0
POOL_WINDOWS = (2, 4, 8, 16)
POOL_GROUP = D // 4
FFN_DIM = 2816
N_EXPERTS = 8
EXPERT_DIM = 3584
EPS = 1e-6

N_LAT = BATCH * SEQ
N_CTX = BATCH * CTX_LEN
N_ALL = N_LAT + N_CTX
LK = CTX_LEN + SEQ
MOD_ROWS = 8
LANES = 128
QK_COLS = (N_HEADS + N_KV) * HEAD_DIM
QKV_COLS = (N_HEADS + 2 * N_KV) * HEAD_DIM

VMEM_LIMIT = 56 * 1024 * 1024

bf16 = jnp.bfloat16
f32 = jnp.float32


def _params(sem, vmem=VMEM_LIMIT):
    return pltpu.CompilerParams(dimension_semantics=sem, vmem_limit_bytes=vmem)


def _mod_row(t, tm):
    start = t * tm
    return jnp.where(start < N_LAT, start // SEQ, BATCH)


def _mod_spec(chunk, tm):
    return pl.BlockSpec((None, None, 1, D), lambda t, *_: (chunk, _mod_row(t, tm), 0, 0))


def _full_spec(shape):
    nd = len(shape)
    return pl.BlockSpec(shape, lambda *_: (0,) * nd)


def _modulate(x, g, shift, scale):
    ms = jnp.mean(x * x, axis=-1, keepdims=True)
    return (x * lax.rsqrt(ms + EPS) * g) * (1.0 + scale) + shift


LOG2E = 1.4426950408889634


def _gelu_tanh(z):
    k0 = -2.0 * 0.7978845608028654 * LOG2E
    return z / (1.0 + jnp.exp2(z * (z * z * (k0 * 0.044715) + k0)))


def _dot(a, b):
    return jnp.dot(a, b, preferred_element_type=f32)


ADA_TN = 1536


def _ada_kernel(cond_ref, w_ref, b_ref, o_ref):
    cond = cond_ref[...]
    s = (cond * jax.nn.sigmoid(cond)).astype(bf16)
    o_ref[...] = _dot(s, w_ref[...].astype(bf16)) + b_ref[...]


def _ada_all(cond, ada_w, ada_b):
    return pl.pallas_call(
        _ada_kernel,
        out_shape=jax.ShapeDtypeStruct((DEPTH, MOD_ROWS, 6 * D), f32),
        grid=(DEPTH, 6 * D // ADA_TN),
        in_specs=[
            _full_spec((MOD_ROWS, D)),
            pl.BlockSpec((None, D, ADA_TN), lambda i, n: (i, 0, n)),
            pl.BlockSpec((None, 1, ADA_TN), lambda i, n: (i, 0, n)),
        ],
        out_specs=pl.BlockSpec((None, MOD_ROWS, ADA_TN), lambda i, n: (i, 0, n)),
        compiler_params=_params(("parallel", "parallel")),
        name="ada",
    )(cond, ada_w, ada_b.reshape(DEPTH, 1, 6 * D))


GMLP_TM = 512


def _gmlp_kernel(x_ref, xc_ref, sh_ref, sc_ref, gt_ref, g_ref, win_ref, vg_ref, ws_ref, bs_ref, wout_ref,
                 o_ref, uv_ref):
    x = jnp.where(pl.program_id(0) < N_LAT // GMLP_TM, x_ref[...], xc_ref[...])
    h = _modulate(x, g_ref[...], sh_ref[...], sc_ref[...]).astype(bf16)
    z = _gelu_tanh(_dot(h, win_ref[...]))
    u = z[:, :A_DIM]
    v = z[:, A_DIM:]
    v = (v * lax.rsqrt(jnp.mean(v * v, axis=-1, keepdims=True) + EPS) * vg_ref[...]).astype(bf16)
    for c in range(GMLP_TM // CHUNK):
        r0 = c * CHUNK
        for g in range(A_GROUPS):
            c0 = g * A_GW
            mixed = _dot(ws_ref[g], v[r0:r0 + CHUNK, c0:c0 + A_GW]) + bs_ref[:, c0:c0 + A_GW]
            uv_ref[r0:r0 + CHUNK, c0:c0 + A_GW] = (u[r0:r0 + CHUNK, c0:c0 + A_GW] * mixed).astype(bf16)
    y = _dot(uv_ref[...], wout_ref[...])
    o_ref[...] = x + gt_ref[...] * y


def _gmlp_mixer(x, xc, with_ctx, mods, g, w_in, v_g, w_s, b_full, w_out, layer):
    rows = N_ALL if with_ctx else N_LAT
    tm = GMLP_TM
    lat_tiles = N_LAT // tm
    return pl.pallas_call(
        _gmlp_kernel,
        out_shape=jax.ShapeDtypeStruct((rows, D), f32),
        grid=(rows // tm,),
        in_specs=[
            pl.BlockSpec((tm, D), lambda t: (jnp.minimum(t, lat_tiles - 1), 0)),
            pl.BlockSpec((tm, D), lambda t: (jnp.maximum(t - lat_tiles, 0), 0)),
            _mod_spec(0, tm), _mod_spec(1, tm), _mod_spec(2, tm),
            _full_spec((1, D)),
            _resident_spec((D, 2 * A_DIM), layer),
            _full_spec((1, A_DIM)),
            _full_spec((A_GROUPS, CHUNK, CHUNK)),
            _full_spec((CHUNK, A_DIM)),
            _resident_spec((A_DIM, D), layer),
        ],
        out_specs=pl.BlockSpec((tm, D), lambda t: (t, 0)),
        scratch_shapes=[pltpu.VMEM((tm, A_DIM), bf16)],
        compiler_params=_params(("parallel",)),
        name="gmlp_mixer",
    )(x, xc, mods, mods, mods, g, w_in, v_g, w_s, b_full, w_out)


FFN_TM = 512


def _ffn_kernel(x_ref, sh_ref, sc_ref, gt_ref, g_ref, wg_ref, wu_ref, wd_ref, o_ref):
    x = x_ref[...]
    h = _modulate(x, g_ref[...], sh_ref[...], sc_ref[...]).astype(bf16)
    a = _dot(h, wg_ref[...])
    b = _dot(h, wu_ref[...])
    act = (a * jax.nn.sigmoid(a) * b).astype(bf16)
    o_ref[...] = x + gt_ref[...] * _dot(act, wd_ref[...])


def _resident_spec(shape, layer):
    nd = len(shape)
    return pl.BlockSpec((None,) + shape, lambda *_: (layer,) + (0,) * nd, pipeline_mode=pl.Buffered(1))


def _dense_ffn(x, mods, g, wg, wu, wd, layer):
    rows = x.shape[0]
    tm = FFN_TM
    return pl.pallas_call(
        _ffn_kernel,
        out_shape=jax.ShapeDtypeStruct((rows, D), f32),
        grid=(rows // tm,),
        in_specs=[
            pl.BlockSpec((tm, D), lambda t: (t, 0)),
            _mod_spec(3, tm), _mod_spec(4, tm), _mod_spec(5, tm),
            _full_spec((1, D)),
            _resident_spec((D, FFN_DIM), layer),
            _resident_spec((D, FFN_DIM), layer),
            _resident_spec((FFN_DIM, D), layer),
        ],
        out_specs=pl.BlockSpec((tm, D), lambda t: (t, 0)),
        compiler_params=_params(("parallel",)),
        name="dense_ffn",
    )(x, mods, mods, mods, g, wg, wu, wd)


QKV_TM = 512
HEADS_PER_VREG = LANES // HEAD_DIM
Q_SCALE = HEAD_DIM ** -0.5 * LOG2E


def _qkv_kernel(x_ref, sh_ref, sc_ref, g_ref, w_ref, gain_ref, bd_ref, cos_ref, sin_ref, o_ref):
    h = _modulate(x_ref[...], g_ref[...], sh_ref[...], sc_ref[...]).astype(bf16)
    qkv = _dot(h, w_ref[...])
    lane = lax.broadcasted_iota(jnp.int32, (QKV_TM, LANES), 1)
    first_half = (lane % (2 * ROPE_FREQS)) < ROPE_FREQS
    cos = cos_ref[...]
    sin = sin_ref[...]
    bd = bd_ref[...]
    for cb in range(QKV_COLS // LANES):
        c0 = cb * LANES
        blk = qkv[:, c0:c0 + LANES]
        if c0 < QK_COLS:
            sq = blk * blk
            hi = sq.astype(bf16)
            lo = (sq - hi.astype(f32)).astype(bf16)
            ms = (_dot(hi, bd) + _dot(lo, bd)) * (1.0 / HEAD_DIM)
            y = blk * lax.rsqrt(ms + EPS) * gain_ref[:, c0:c0 + LANES]
            partner = jnp.where(first_half, pltpu.roll(y, LANES - ROPE_FREQS, 1),
                                pltpu.roll(y, ROPE_FREQS, 1))
            y = y * cos + partner * sin
            if c0 < N_HEADS * HEAD_DIM:
                y = y * Q_SCALE
            blk = y
        o_ref[:, c0:c0 + LANES] = blk.astype(bf16)


def _qkv_proj(x_all, mods, g, w_qkv, gain, bd, cos_t, sin_t):
    tm = QKV_TM
    tiles_per_seq = SEQ // tm

    def tab_idx(t):
        return (jnp.where(t * tm < N_LAT, t % tiles_per_seq, tiles_per_seq), 0)

    return pl.pallas_call(
        _qkv_kernel,
        out_shape=jax.ShapeDtypeStruct((N_ALL, QKV_COLS), bf16),
        grid=(N_ALL // tm,),
        in_specs=[
            pl.BlockSpec((tm, D), lambda t: (t, 0)),
            _mod_spec(0, tm), _mod_spec(1, tm),
            _full_spec((1, D)),
            _full_spec((D, QKV_COLS)),
            _full_spec((1, QK_COLS)),
            _full_spec((LANES, LANES)),
            pl.BlockSpec((tm, LANES), tab_idx),
            pl.BlockSpec((tm, LANES), tab_idx),
        ],
        out_specs=pl.BlockSpec((tm, QKV_COLS), lambda t: (t, 0)),
        compiler_params=_params(("parallel",)),
        name="qkv_proj",
    )(x_all, mods, mods, g, w_qkv, gain, bd, cos_t, sin_t)


ATT_TQ = 512
REP = N_HEADS // N_KV
ATT_GROUPS = 2
ATT_HEADS = ATT_GROUPS * REP
ATT_KC = 256
ATT_LAG = 2


def _attn_kernel(qt_ref, kk_ref, vt_ref, o_ref, *st_refs):
    row = lax.broadcasted_iota(jnp.int32, (LANES, ATT_TQ), 0)
    low = row < HEAD_DIM
    zero = jnp.zeros((LANES, ATT_TQ), bf16)
    n_chunks = LK // ATT_KC
    sub = 8
    n_buf = len(st_refs)

    def masked_q(h):
        q2 = qt_ref[(h // HEADS_PER_VREG) * LANES:(h // HEADS_PER_VREG + 1) * LANES, :]
        return jnp.where(low if h % HEADS_PER_VREG == 0 else ~low, q2, zero)

    col_max = {}
    for ph in range(ATT_HEADS + ATT_LAG):
        ha, hb = ph, ph - ATT_LAG
        qm = masked_q(ha) if ha < ATT_HEADS else None
        mx = None
        acc = None
        for c in range(n_chunks):
            rows = slice(c * ATT_KC, (c + 1) * ATT_KC)
            if ha < ATT_HEADS:
                st = _dot(kk_ref[ha // REP, rows, :], qm)
                st_refs[ha % n_buf][rows, :] = st
                cm = jnp.max(st.reshape(ATT_KC // sub, sub, ATT_TQ), axis=0)
                mx = cm if mx is None else jnp.maximum(mx, cm)
            if hb >= 0:
                pt = jnp.exp2(st_refs[hb % n_buf][rows, :] - col_max[hb]).astype(bf16)
                part = _dot(vt_ref[hb // REP, :, rows], pt)
                acc = part if acc is None else acc + part
        if hb >= 0:
            o_ref[hb * HEAD_DIM:(hb + 1) * HEAD_DIM, :] = (
                acc[:HEAD_DIM] / acc[HEAD_DIM:HEAD_DIM + 1]).astype(bf16)
        if ha < ATT_HEADS:
            col_max[ha] = jnp.max(mx, axis=0, keepdims=True)


def _attention(qt, kk, vt):
    tq = ATT_TQ
    qt_per_seq = SEQ // tq
    gw = ATT_HEADS * HEAD_DIM
    return pl.pallas_call(
        _attn_kernel,
        out_shape=jax.ShapeDtypeStruct((D, N_LAT), bf16),
        grid=(BATCH, N_KV // ATT_GROUPS, qt_per_seq),
        in_specs=[
            pl.BlockSpec((gw, tq), lambda b, g, t: (g, b * qt_per_seq + t)),
            pl.BlockSpec((None, ATT_GROUPS, LK, LANES), lambda b, g, t: (b, g, 0, 0)),
            pl.BlockSpec((None, ATT_GROUPS, LANES, LK), lambda b, g, t: (b, g, 0, 0)),
        ],
        out_specs=pl.BlockSpec((gw, tq), lambda b, g, t: (g, b * qt_per_seq + t)),
        scratch_shapes=[pltpu.VMEM((LK, tq), f32)] * (ATT_LAG + 1),
        compiler_params=_params(("parallel", "parallel", "arbitrary")),
        name="attention",
    )(qt, kk, vt)


PROJ_TM = 512


def _oproj_kernel(o_ref, x_ref, gt_ref, w_ref, out_ref):
    out_ref[...] = x_ref[...] + gt_ref[...] * _dot(o_ref[...], w_ref[...])


def _attn_out(o, x_all, mods, w_o):
    tm = PROJ_TM
    return pl.pallas_call(
        _oproj_kernel,
        out_shape=jax.ShapeDtypeStruct((N_LAT, D), f32),
        grid=(N_LAT // tm,),
        in_specs=[
            pl.BlockSpec((tm, D), lambda t: (t, 0)),
            pl.BlockSpec((tm, D), lambda t: (t, 0)),
            _mod_spec(2, tm),
            _full_spec((D, D)),
        ],
        out_specs=pl.BlockSpec((tm, D), lambda t: (t, 0)),
        compiler_params=_params(("parallel",)),
        name="attn_out",
    )(o, x_all, mods, w_o)


POOL_TM = 512
POOL_HALO = 8
POOL_EXT = POOL_TM + 2 * POOL_HALO
assert max(POOL_WINDOWS) // 2 <= POOL_HALO


def _pool_kernel(xp_ref, x_ref, xn_ref, sh_ref, sc_ref, gt_ref, g_ref, pw_ref, ps_ref, o_ref):
    t = pl.program_id(0)
    tiles_per_seq = SEQ // POOL_TM
    ts = t % tiles_per_seq
    x = x_ref[...]
    g, sh, sc = g_ref[...], sh_ref[...], sc_ref[...]
    h = _modulate(x, g, sh, sc)
    hp = jnp.where(ts > 0, _modulate(xp_ref[...], g, sh, sc), 0.0)
    hn = jnp.where(ts < tiles_per_seq - 1, _modulate(xn_ref[...], g, sh, sc), 0.0)
    ext = jnp.concatenate([hp, h, hn], axis=0)
    pos = ts * POOL_TM + lax.broadcasted_iota(jnp.int32, (POOL_TM, 1), 0)
    for j, w in enumerate(POOL_WINDOWS):
        c0 = j * POOL_GROUP
        p = ext[:, c0:c0 + POOL_GROUP]
        k = 1
        while k < w:
            p = p + pltpu.roll(p, POOL_EXT - k, 0)
            k *= 2
        first = POOL_HALO - w // 2
        s = (pltpu.roll(p, POOL_EXT - first, 0) if first else p)[:POOL_TM]
        lo_i = jnp.maximum(pos - w // 2, 0)
        hi_i = jnp.minimum(pos + w // 2 - 1, SEQ - 1)
        cnt = (hi_i - lo_i + 1).astype(f32)
        d = (s / cnt - h[:, c0:c0 + POOL_GROUP]).astype(bf16)
        y = _dot(d, pw_ref[j]) * ps_ref[:, c0:c0 + POOL_GROUP]
        o_ref[:, c0:c0 + POOL_GROUP] = x[:, c0:c0 + POOL_GROUP] + gt_ref[:, c0:c0 + POOL_GROUP] * y


def _pool_mixer(x, mods, g, p_w, p_scale):
    tm = POOL_TM
    hb = tm // POOL_HALO
    n_halo_blocks = N_LAT // POOL_HALO
    return pl.pallas_call(
        _pool_kernel,
        out_shape=jax.ShapeDtypeStruct((N_LAT, D), f32),
        grid=(N_LAT // tm,),
        in_specs=[
            pl.BlockSpec((POOL_HALO, D), lambda t: (jnp.maximum(t * hb - 1, 0), 0)),
            pl.BlockSpec((tm, D), lambda t: (t, 0)),
            pl.BlockSpec((POOL_HALO, D), lambda t: (jnp.minimum((t + 1) * hb, n_halo_blocks - 1), 0)),
            _mod_spec(0, tm), _mod_spec(1, tm), _mod_spec(2, tm),
            _full_spec((1, D)),
            _full_spec((len(POOL_WINDOWS), POOL_GROUP, POOL_GROUP)),
            _full_spec((1, D)),
        ],
        out_specs=pl.BlockSpec((tm, D), lambda t: (t, 0)),
        compiler_params=_params(("parallel",)),
        name="pool_mixer",
    )(x, x, x, mods, mods, mods, g, p_w, p_scale)


ROUTER_TM = 512


def _router_kernel(x_ref, sh_ref, sc_ref, g_ref, wr_ref, ltri_ref, h_ref, route_ref, cnt_ref, run_ref):
    @pl.when(pl.program_id(0) == 0)
    def _():
        run_ref[...] = jnp.zeros_like(run_ref)

    h = _modulate(x_ref[...], g_ref[...], sh_ref[...], sc_ref[...])
    h_ref[...] = h
    h_hi = h.astype(bf16)
    h_lo = (h - h_hi.astype(f32)).astype(bf16)
    p1 = _dot(h_hi, wr_ref[0])
    logits = p1 + pltpu.roll(p1, LANES - N_EXPERTS, 1) + _dot(h_lo, wr_ref[1])
    lane = lax.broadcasted_iota(jnp.int32, logits.shape, 1)
    neg = jnp.float32(-jnp.inf)
    logits = jnp.where(lane < N_EXPERTS, logits, neg)
    m1 = jnp.max(logits, axis=-1, keepdims=True)
    i1 = jnp.min(jnp.where(logits == m1, lane, LANES), axis=-1, keepdims=True)
    rest = jnp.where(lane == i1, neg, logits)
    m2 = jnp.max(rest, axis=-1, keepdims=True)
    i2 = jnp.min(jnp.where(rest == m2, lane, LANES), axis=-1, keepdims=True)
    e2 = jnp.exp(m2 - m1)
    g1 = 1.0 / (1.0 + e2)
    g2 = e2 / (1.0 + e2)
    oh1 = jnp.where(lane == i1, 1.0, 0.0)
    oh2 = jnp.where(lane == i2, 1.0, 0.0)
    ltri = ltri_ref[...]
    before1 = _dot(ltri, oh1.astype(bf16))
    before2 = _dot(ltri, oh2.astype(bf16))
    tot1 = jnp.sum(oh1, axis=0, keepdims=True)
    tot2 = jnp.sum(oh2, axis=0, keepdims=True)
    run = run_ref[...]
    lp1 = jnp.sum(oh1 * (run + before1), axis=-1, keepdims=True)
    lp2 = jnp.sum(oh2 * (run + tot1 + before2), axis=-1, keepdims=True)
    run = run + tot1 + tot2
    run_ref[...] = run
    cnt_ref[...] = run
    route = jnp.where(lane == 0, i1.astype(f32),
                      jnp.where(lane == 1, i2.astype(f32),
                                jnp.where(lane == 2, g1,
                                          jnp.where(lane == 3, g2,
                                                    jnp.where(lane == 4, lp1, jnp.where(lane == 5, lp2, 0.0))))))
    route_ref[...] = route


def _router(x, mods, g, wr_pad):
    tm = ROUTER_TM
    ltri = jnp.asarray(np.tril(np.ones((tm, tm), np.float32), -1), dtype=bf16)
    return pl.pallas_call(
        _router_kernel,
        out_shape=(jax.ShapeDtypeStruct((N_LAT, D), f32),
                   jax.ShapeDtypeStruct((N_LAT, LANES), f32),
                   jax.ShapeDtypeStruct((1, LANES), f32)),
        grid=(N_LAT // tm,),
        in_specs=[
            pl.BlockSpec((tm, D), lambda t: (t, 0)),
            _mod_spec(3, tm), _mod_spec(4, tm),
            _full_spec((1, D)),
            _full_spec((2, D, LANES)),
            _full_spec((tm, tm)),
        ],
        out_specs=(pl.BlockSpec((tm, D), lambda t: (t, 0)),
                   pl.BlockSpec((tm, LANES), lambda t: (t, 0)),
                   _full_spec((1, LANES))),
        scratch_shapes=[pltpu.VMEM((1, LANES), f32)],
        compiler_params=_params(("arbitrary",)),
        name="router",
    )(x, mods, mods, g, wr_pad, ltri)


CAST_BLOCK_BYTES = 8 * 1024 * 1024


def _cast_kernel(w_ref, o_ref):
    o_ref[...] = w_ref[...].astype(bf16)


def _layer_weights_bf16(w, layer):
    n_layers, n_e, r, c = w.shape
    rows = n_e * r
    block_rows = 1 << ((CAST_BLOCK_BYTES // (4 * c)).bit_length() - 1)
    assert rows % block_rows == 0
    steps = rows // block_rows
    out = pl.pallas_call(
        _cast_kernel,
        out_shape=jax.ShapeDtypeStruct((rows, c), bf16),
        grid=(steps,),
        in_specs=[pl.BlockSpec((block_rows, c), lambda i: (layer * steps + i, 0))],
        out_specs=pl.BlockSpec((block_rows, c), lambda i: (i, 0)),
        compiler_params=_params(("parallel",)),
        name="cast_expert_weights",
    )(w.reshape(n_layers * rows, c))
    return out.reshape(n_e, r, c)


MOE_TM = 512
MOE_TF = 1792
MOE_TILES = (2 * N_LAT) // MOE_TM + N_EXPERTS
MOE_ROWS = MOE_TILES * MOE_TM
MOE_XS_PARTS = 3
MOE_PART_TILES = MOE_TILES // MOE_XS_PARTS
assert MOE_PART_TILES * MOE_XS_PARTS == MOE_TILES


def _expert_kernel(te_ref, nt_ref, *refs):
    xs_refs = refs[:MOE_XS_PARTS]
    wg_ref, wu_ref, wd_ref, y_ref, xb_ref, acc_ref = refs[MOE_XS_PARTS:]
    t = pl.program_id(0)
    j = pl.program_id(1)
    last = pl.num_programs(1) - 1
    active = t < nt_ref[0]

    @pl.when(active)
    def _():
        @pl.when(j == 0)
        def _():
            for k, xs_ref in enumerate(xs_refs):
                @pl.when(t // MOE_PART_TILES == k)
                def _():
                    xb_ref[...] = xs_ref[...].astype(bf16)
            acc_ref[...] = jnp.zeros_like(acc_ref)

        xs = xb_ref[...]
        a = _dot(xs, wg_ref[...])
        b = _dot(xs, wu_ref[...])
        act = (a * jax.nn.sigmoid(a) * b).astype(bf16)
        acc_ref[...] += _dot(act, wd_ref[...])

        @pl.when(j == last)
        def _():
            y_ref[...] = acc_ref[...]

    @pl.when(jnp.logical_not(active) & (j == last))
    def _():
        y_ref[...] = jnp.zeros_like(y_ref)


def _experts(tile_expert, num_tiles, xs_parts, wg, wu, wd):
    tm, tf = MOE_TM, MOE_TF
    nj = EXPERT_DIM // tf

    def xs_spec(k):
        def index(t, j, te, nt):
            tt = jnp.minimum(t, jnp.maximum(nt[0] - 1, 0))
            return (jnp.clip(tt - k * MOE_PART_TILES, 0, MOE_PART_TILES - 1), 0)
        return pl.BlockSpec((tm, D), index)

    def w_col(t, j, te, nt):
        jj = jnp.where(t < nt[0], j, nj - 1)
        return (te[t], 0, jj)

    def w_row(t, j, te, nt):
        jj = jnp.where(t < nt[0], j, nj - 1)
        return (te[t], jj, 0)

    return pl.pallas_call(
        _expert_kernel,
        out_shape=jax.ShapeDtypeStruct((MOE_ROWS, D), f32),
        grid_spec=pltpu.PrefetchScalarGridSpec(
            num_scalar_prefetch=2,
            grid=(MOE_TILES, nj),
            in_specs=[xs_spec(k) for k in range(MOE_XS_PARTS)] + [
                pl.BlockSpec((None, D, tf), w_col),
                pl.BlockSpec((None, D, tf), w_col),
                pl.BlockSpec((None, tf, D), w_row),
            ],
            out_specs=pl.BlockSpec((tm, D), lambda t, j, te, nt: (t, 0)),
            scratch_shapes=[pltpu.VMEM((tm, D), bf16), pltpu.VMEM((tm, D), f32)],
        ),
        compiler_params=_params(("arbitrary", "arbitrary")),
        name="moe_experts",
    )(tile_expert, num_tiles, *xs_parts, wg, wu, wd)


COMBINE_TM = 512


def _combine_kernel(x_ref, y1_ref, y2_ref, route_ref, gt_ref, o_ref):
    route = route_ref[...]
    mix = route[:, 2:3] * y1_ref[...] + route[:, 3:4] * y2_ref[...]
    o_ref[...] = x_ref[...] + gt_ref[...] * mix


def _combine(x, y1, y2, route, mods):
    tm = COMBINE_TM
    return pl.pallas_call(
        _combine_kernel,
        out_shape=jax.ShapeDtypeStruct((N_LAT, D), f32),
        grid=(N_LAT // tm,),
        in_specs=[
            pl.BlockSpec((tm, D), lambda t: (t, 0)),
            pl.BlockSpec((tm, D), lambda t: (t, 0)),
            pl.BlockSpec((tm, D), lambda t: (t, 0)),
            pl.BlockSpec((tm, LANES), lambda t: (t, 0)),
            _mod_spec(5, tm),
        ],
        out_specs=pl.BlockSpec((tm, D), lambda t: (t, 0)),
        compiler_params=_params(("parallel",)),
        name="moe_combine",
    )(x, y1, y2, route, mods)


def _expert_table(table, e):
    ids = jnp.arange(N_EXPERTS, dtype=jnp.int32)
    return jnp.sum(jnp.where(e[:, None] == ids[None, :], table[None, :], 0), axis=1)


def _take_rows(a, idx):
    return a.at[idx].get(mode="promise_in_bounds")


def _moe(x, mods, g, wr_pad, wg, wu, wd, expert_base):
    tm = MOE_TM
    hp, route, cnt = _router(x, mods, g, wr_pad)
    e1 = route[:, 0].astype(jnp.int32)
    e2 = route[:, 1].astype(jnp.int32)
    counts = cnt[0, :N_EXPERTS].astype(jnp.int32)
    tiles_per_e = (counts + tm - 1) // tm
    tile_end = jnp.cumsum(tiles_per_e)
    group_base = (tile_end - tiles_per_e) * tm
    dense_base = jnp.cumsum(counts) - counts
    pos1 = _expert_table(group_base, e1) + route[:, 4].astype(jnp.int32)
    pos2 = _expert_table(group_base, e2) + route[:, 5].astype(jnp.int32)
    num_tiles = tile_end[-1]
    tile_ids = jnp.arange(MOE_TILES, dtype=jnp.int32)
    tile_expert = jnp.sum(tile_end[None, :] <= jnp.minimum(tile_ids, num_tiles - 1)[:, None], axis=1)
    tile_expert = jnp.minimum(tile_expert, N_EXPERTS - 1).astype(jnp.int32)
    tok = jnp.arange(N_LAT, dtype=jnp.int32)
    _, sorted_tok = lax.sort_key_val(jnp.concatenate([pos1, pos2]), jnp.concatenate([tok, tok]))
    row_expert = jnp.repeat(tile_expert, tm)
    src = jnp.arange(MOE_ROWS, dtype=jnp.int32) - _expert_table(group_base - dense_base, row_expert)
    row_token = _take_rows(sorted_tok, jnp.clip(src, 0, 2 * N_LAT - 1))
    part_rows = MOE_PART_TILES * tm
    xs_parts = [_take_rows(hp, row_token[k * part_rows:(k + 1) * part_rows]) for k in range(MOE_XS_PARTS)]
    y = _experts(tile_expert + expert_base, num_tiles.reshape(1).astype(jnp.int32), xs_parts, wg, wu, wd)
    return _combine(x, _take_rows(y, pos1), _take_rows(y, pos2), route, mods)


def _rope_tables():
    rows = SEQ // GRID_W
    r = jnp.repeat(jnp.arange(rows), GRID_W)
    col = jnp.tile(jnp.arange(GRID_W), rows)
    inv = ROPE_THETA ** (-jnp.arange(ROPE_FREQS, dtype=f32) / ROPE_FREQS)
    ang = jnp.stack([r, col], axis=-1).astype(f32)[..., None] * inv
    c = jnp.cos(ang)
    s = jnp.sin(ang)
    cos_h = jnp.concatenate([c[:, 0], c[:, 0], c[:, 1], c[:, 1]], axis=-1)
    sin_h = jnp.concatenate([-s[:, 0], s[:, 0], -s[:, 1], s[:, 1]], axis=-1)
    cos_t = jnp.tile(cos_h, (1, HEADS_PER_VREG))
    sin_t = jnp.tile(sin_h, (1, HEADS_PER_VREG))
    cos_t = jnp.concatenate([cos_t, jnp.ones((QKV_TM, LANES), f32)], axis=0)
    sin_t = jnp.concatenate([sin_t, jnp.zeros((QKV_TM, LANES), f32)], axis=0)
    return cos_t, sin_t


def kernel(x, c, ctx, c_ctx, ada_w, ada_b, norm_g, a_w_in, a_v_g, a_ws, a_bs, a_w_out, b_w_qkv, b_q_g, b_k_g,
           b_w_o, p_w, p_scale, f_w_gate, f_w_up, f_w_down, m_router, m_w_gate, m_w_up, m_w_down):
    cond = jnp.concatenate([c, c_ctx[None, :], jnp.zeros((MOD_ROWS - BATCH - 1, D), f32)], axis=0)
    ada = _ada_all(cond, ada_w, ada_b)
    mods = ada.reshape(DEPTH, MOD_ROWS, 6, 1, D).transpose(0, 2, 1, 3, 4)

    gmlp_w_in = a_w_in.astype(bf16)
    gmlp_w_out = a_w_out.astype(bf16)

    def gmlp_weights(j):
        b_full = jnp.repeat(a_bs[j].T, A_GW, axis=1)
        return gmlp_w_in, a_v_g[j][None, :], a_ws[j].astype(bf16), b_full, gmlp_w_out, j

    def moe_weights(f):
        w_hi = m_router[f].astype(bf16)
        w_lo = (m_router[f] - w_hi.astype(f32)).astype(bf16)
        zeros = jnp.zeros((D, LANES - 2 * N_EXPERTS), bf16)
        wr = jnp.stack([jnp.concatenate([w_hi, w_lo, zeros], axis=1),
                        jnp.concatenate([w_hi, jnp.zeros_like(w_lo), zeros], axis=1)])
        return (wr, _layer_weights_bf16(m_w_gate, f), _layer_weights_bf16(m_w_up, f),
                _layer_weights_bf16(m_w_down, f), 0)

    x_lat = x.reshape(N_LAT, D)
    x_ctx = ctx.reshape(N_CTX, D)

    x_all = _gmlp_mixer(x_lat, x_ctx, True, mods[0], norm_g[0, 0][None, :], *gmlp_weights(0))
    ffn_w = (f_w_gate.astype(bf16), f_w_up.astype(bf16), f_w_down.astype(bf16))
    x_all = _dense_ffn(x_all, mods[0], norm_g[0, 1][None, :], *ffn_w, 0)

    gain = jnp.concatenate([jnp.tile(b_q_g[0], N_HEADS), jnp.tile(b_k_g[0], N_KV)])[None, :]
    head_of_lane = np.arange(LANES) // HEAD_DIM
    bd = jnp.asarray((head_of_lane[:, None] == head_of_lane[None, :]).astype(np.float32), dtype=bf16)
    cos_t, sin_t = _rope_tables()
    qkv = _qkv_proj(x_all, mods[1], norm_g[1, 0][None, :], b_w_qkv[0].astype(bf16), gain, bd, cos_t, sin_t)

    def keys_first(a):
        return jnp.concatenate([a[N_LAT:].reshape(BATCH, CTX_LEN, -1), a[:N_LAT].reshape(BATCH, SEQ, -1)], axis=1)

    k = keys_first(qkv[:, N_HEADS * HEAD_DIM:QK_COLS]).reshape(BATCH, LK, N_KV, HEAD_DIM)
    v = keys_first(qkv[:, QK_COLS:]).reshape(BATCH, LK, N_KV, HEAD_DIM)
    kh = k.transpose(0, 2, 1, 3)
    kk = jnp.concatenate([kh] * HEADS_PER_VREG, axis=3)
    vt = v.transpose(0, 2, 3, 1)
    vt = jnp.concatenate([vt, jnp.ones((BATCH, N_KV, LANES - HEAD_DIM, LK), bf16)], axis=2)
    qt = qkv[:N_LAT, :N_HEADS * HEAD_DIM].T
    o = _attention(qt, kk, vt).T
    xl = _attn_out(o, x_all, mods[1], b_w_o[0].astype(bf16))
    xl = _moe(xl, mods[1], norm_g[1, 1][None, :], *moe_weights(0))

    xl = _pool_mixer(xl, mods[2], norm_g[2, 0][None, :], p_w[0].astype(bf16), p_scale[0][None, :])
    xl = _dense_ffn(xl, mods[2], norm_g[2, 1][None, :], *ffn_w, 1)

    xl = _gmlp_mixer(xl, x_ctx, False, mods[3], norm_g[3, 0][None, :], *gmlp_weights(1))
    xl = _moe(xl, mods[3], norm_g[3, 1][None, :], *moe_weights(1))
    return xl.reshape(BATCH, SEQ, D)
```

```python
import functools

import jax
import jax.numpy as jnp
import numpy as np
from jax import lax
from jax.experimental import pallas as pl
from jax.experimental.pallas import tpu as pltpu

D = 1024
BATCH = 4
SEQ = 4096
DEPTH = 4
GRID_W = 64
CTX_LEN = 256
CHUNK = 128
A_DIM = 2 * D
A_GROUPS = 8
A_GW = A_DIM // A_GROUPS
N_HEADS = 16
N_KV = 4
HEAD_DIM = 64
ROPE_FREQS = HEAD_DIM // 4
ROPE_THETA = 10000.0
POOL_WINDOWS = (2, 4, 8, 16)
POOL_GROUP = D // 4
FFN_DIM = 2816
N_EXPERTS = 8
EXPERT_DIM = 3584
EPS = 1e-6

N_LAT = BATCH * SEQ
N_CTX = BATCH * CTX_LEN
N_ALL = N_LAT + N_CTX
LK = CTX_LEN + SEQ
MOD_ROWS = 8
LANES = 128
QK_COLS = (N_HEADS + N_KV) * HEAD_DIM
QKV_COLS = (N_HEADS + 2 * N_KV) * HEAD_DIM

VMEM_LIMIT = 56 * 1024 * 1024

bf16 = jnp.bfloat16
f32 = jnp.float32


def _params(sem, vmem=VMEM_LIMIT):
    return pltpu.CompilerParams(dimension_semantics=sem, vmem_limit_bytes=vmem)


def _mod_row(t, tm):
    start = t * tm
    return jnp.where(start < N_LAT, start // SEQ, BATCH)


def _mod_spec(chunk, tm):
    return pl.BlockSpec((None, None, 1, D), lambda t, *_: (chunk, _mod_row(t, tm), 0, 0))


def _full_spec(shape):
    nd = len(shape)
    return pl.BlockSpec(shape, lambda *_: (0,) * nd)


def _modulate(x, g, shift, scale):
    ms = jnp.mean(x * x, axis=-1, keepdims=True)
    return (x * lax.rsqrt(ms + EPS) * g) * (1.0 + scale) + shift


LOG2E = 1.4426950408889634


def _gelu_tanh(z):
    k0 = -2.0 * 0.7978845608028654 * LOG2E
    return z / (1.0 + jnp.exp2(z * (z * z * (k0 * 0.044715) + k0)))


def _dot(a, b):
    return jnp.dot(a, b, preferred_element_type=f32)


ADA_TN = 1536


def _ada_kernel(cond_ref, w_ref, b_ref, o_ref):
    cond = cond_ref[...]
    s = (cond * jax.nn.sigmoid(cond)).astype(bf16)
    o_ref[...] = _dot(s, w_ref[...].astype(bf16)) + b_ref[...]


def _ada_all(cond, ada_w, ada_b):
    return pl.pallas_call(
        _ada_kernel,
        out_shape=jax.ShapeDtypeStruct((DEPTH, MOD_ROWS, 6 * D), f32),
        grid=(DEPTH, 6 * D // ADA_TN),
        in_specs=[
            _full_spec((MOD_ROWS, D)),
            pl.BlockSpec((None, D, ADA_TN), lambda i, n: (i, 0, n)),
            pl.BlockSpec((None, 1, ADA_TN), lambda i, n: (i, 0, n)),
        ],
        out_specs=pl.BlockSpec((None, MOD_ROWS, ADA_TN), lambda i, n: (i, 0, n)),
        compiler_params=_params(("parallel", "parallel")),
        name="ada",
    )(cond, ada_w, ada_b.reshape(DEPTH, 1, 6 * D))


GMLP_TM = 512


def _gmlp_kernel(x_ref, xc_ref, sh_ref, sc_ref, gt_ref, g_ref, win_ref, vg_ref, ws_ref, bs_ref, wout_ref,
                 o_ref, uv_ref):
    x = jnp.where(pl.program_id(0) < N_LAT // GMLP_TM, x_ref[...], xc_ref[...])
    h = _modulate(x, g_ref[...], sh_ref[...], sc_ref[...]).astype(bf16)
    z = _gelu_tanh(_dot(h, win_ref[...]))
    u = z[:, :A_DIM]
    v = z[:, A_DIM:]
    v = (v * lax.rsqrt(jnp.mean(v * v, axis=-1, keepdims=True) + EPS) * vg_ref[...]).astype(bf16)
    for c in range(GMLP_TM // CHUNK):
        r0 = c * CHUNK
        for g in range(A_GROUPS):
            c0 = g * A_GW
            mixed = _dot(ws_ref[g], v[r0:r0 + CHUNK, c0:c0 + A_GW]) + bs_ref[:, c0:c0 + A_GW]
            uv_ref[r0:r0 + CHUNK, c0:c0 + A_GW] = (u[r0:r0 + CHUNK, c0:c0 + A_GW] * mixed).astype(bf16)
    y = _dot(uv_ref[...], wout_ref[...])
    o_ref[...] = x + gt_ref[...] * y


def _gmlp_mixer(x, xc, with_ctx, mods, g, w_in, v_g, w_s, b_full, w_out, layer):
    rows = N_ALL if with_ctx else N_LAT
    tm = GMLP_TM
    lat_tiles = N_LAT // tm
    return pl.pallas_call(
        _gmlp_kernel,
        out_shape=jax.ShapeDtypeStruct((rows, D), f32),
        grid=(rows // tm,),
        in_specs=[
            pl.BlockSpec((tm, D), lambda t: (jnp.minimum(t, lat_tiles - 1), 0)),
            pl.BlockSpec((tm, D), lambda t: (jnp.maximum(t - lat_tiles, 0), 0)),
            _mod_spec(0, tm), _mod_spec(1, tm), _mod_spec(2, tm),
            _full_spec((1, D)),
            _resident_spec((D, 2 * A_DIM), layer),
            _full_spec((1, A_DIM)),
            _full_spec((A_GROUPS, CHUNK, CHUNK)),
            _full_spec((CHUNK, A_DIM)),
            _resident_spec((A_DIM, D), layer),
        ],
        out_specs=pl.BlockSpec((tm, D), lambda t: (t, 0)),
        scratch_shapes=[pltpu.VMEM((tm, A_DIM), bf16)],
        compiler_params=_params(("parallel",)),
        name="gmlp_mixer",
    )(x, xc, mods, mods, mods, g, w_in, v_g, w_s, b_full, w_out)


FFN_TM = 512


def _ffn_kernel(x_ref, sh_ref, sc_ref, gt_ref, g_ref, wg_ref, wu_ref, wd_ref, o_ref):
    x = x_ref[...]
    h = _modulate(x, g_ref[...], sh_ref[...], sc_ref[...]).astype(bf16)
    a = _dot(h, wg_ref[...])
    b = _dot(h, wu_ref[...])
    act = (a * jax.nn.sigmoid(a) * b).astype(bf16)
    o_ref[...] = x + gt_ref[...] * _dot(act, wd_ref[...])


def _resident_spec(shape, layer):
    nd = len(shape)
    return pl.BlockSpec((None,) + shape, lambda *_: (layer,) + (0,) * nd, pipeline_mode=pl.Buffered(1))


def _dense_ffn(x, mods, g, wg, wu, wd, layer):
    rows = x.shape[0]
    tm = FFN_TM
    return pl.pallas_call(
        _ffn_kernel,
        out_shape=jax.ShapeDtypeStruct((rows, D), f32),
        grid=(rows // tm,),
        in_specs=[
            pl.BlockSpec((tm, D), lambda t: (t, 0)),
            _mod_spec(3, tm), _mod_spec(4, tm), _mod_spec(5, tm),
            _full_spec((1, D)),
            _resident_spec((D, FFN_DIM), layer),
            _resident_spec((D, FFN_DIM), layer),
            _resident_spec((FFN_DIM, D), layer),
        ],
        out_specs=pl.BlockSpec((tm, D), lambda t: (t, 0)),
        compiler_params=_params(("parallel",)),
        name="dense_ffn",
    )(x, mods, mods, mods, g, wg, wu, wd)


QKV_TM = 512
HEADS_PER_VREG = LANES // HEAD_DIM
Q_SCALE = HEAD_DIM ** -0.5 * LOG2E


def _qkv_kernel(x_ref, sh_ref, sc_ref, g_ref, w_ref, gain_ref, bd_ref, cos_ref, sin_ref, o_ref):
    h = _modulate(x_ref[...], g_ref[...], sh_ref[...], sc_ref[...]).astype(bf16)
    qkv = _dot(h, w_ref[...])
    lane = lax.broadcasted_iota(jnp.int32, (QKV_TM, LANES), 1)
    first_half = (lane % (2 * ROPE_FREQS)) < ROPE_FREQS
    cos = cos_ref[...]
    sin = sin_ref[...]
    bd = bd_ref[...]
    for cb in range(QKV_COLS // LANES):
        c0 = cb * LANES
        blk = qkv[:, c0:c0 + LANES]
        if c0 < QK_COLS:
            sq = blk * blk
            hi = sq.astype(bf16)
            lo = (sq - hi.astype(f32)).astype(bf16)
            ms = (_dot(hi, bd) + _dot(lo, bd)) * (1.0 / HEAD_DIM)
            y = blk * lax.rsqrt(ms + EPS) * gain_ref[:, c0:c0 + LANES]
            partner = jnp.where(first_half, pltpu.roll(y, LANES - ROPE_FREQS, 1),
                                pltpu.roll(y, ROPE_FREQS, 1))
            y = y * cos + partner * sin
            if c0 < N_HEADS * HEAD_DIM:
                y = y * Q_SCALE
            blk = y
        o_ref[:, c0:c0 + LANES] = blk.astype(bf16)


def _qkv_proj(x_all, mods, g, w_qkv, gain, bd, cos_t, sin_t):
    tm = QKV_TM
    tiles_per_seq = SEQ // tm

    def tab_idx(t):
        return (jnp.where(t * tm < N_LAT, t % tiles_per_seq, tiles_per_seq), 0)

    return pl.pallas_call(
        _qkv_kernel,
        out_shape=jax.ShapeDtypeStruct((N_ALL, QKV_COLS), bf16),
        grid=(N_ALL // tm,),
        in_specs=[
            pl.BlockSpec((tm, D), lambda t: (t, 0)),
            _mod_spec(0, tm), _mod_spec(1, tm),
            _full_spec((1, D)),
            _full_spec((D, QKV_COLS)),
            _full_spec((1, QK_COLS)),
            _full_spec((LANES, LANES)),
            pl.BlockSpec((tm, LANES), tab_idx),
            pl.BlockSpec((tm, LANES), tab_idx),
        ],
        out_specs=pl.BlockSpec((tm, QKV_COLS), lambda t: (t, 0)),
        compiler_params=_params(("parallel",)),
        name="qkv_proj",
    )(x_all, mods, mods, g, w_qkv, gain, bd, cos_t, sin_t)


ATT_TQ = 512
REP = N_HEADS // N_KV
ATT_GROUPS = 2
ATT_HEADS = ATT_GROUPS * REP
ATT_KC = 256
ATT_LAG = 2


def _attn_kernel(qt_ref, kk_ref, vt_ref, o_ref, *st_refs):
    row = lax.broadcasted_iota(jnp.int32, (LANES, ATT_TQ), 0)
    low = row < HEAD_DIM
    zero = jnp.zeros((LANES, ATT_TQ), bf16)
    n_chunks = LK // ATT_KC
    sub = 8
    n_buf = len(st_refs)

    def masked_q(h):
        q2 = qt_ref[(h // HEADS_PER_VREG) * LANES:(h // HEADS_PER_VREG + 1) * LANES, :]
        return jnp.where(low if h % HEADS_PER_VREG == 0 else ~low, q2, zero)

    col_max = {}
    for ph in range(ATT_HEADS + ATT_LAG):
        ha, hb = ph, ph - ATT_LAG
        qm = masked_q(ha) if ha < ATT_HEADS else None
        mx = None
        acc = None
        for c in range(n_chunks):
            rows = slice(c * ATT_KC, (c + 1) * ATT_KC)
            if ha < ATT_HEADS:
                st = _dot(kk_ref[ha // REP, rows, :], qm)
                st_refs[ha % n_buf][rows, :] = st
                cm = jnp.max(st.reshape(ATT_KC // sub, sub, ATT_TQ), axis=0)
                mx = cm if mx is None else jnp.maximum(mx, cm)
            if hb >= 0:
                pt = jnp.exp2(st_refs[hb % n_buf][rows, :] - col_max[hb]).astype(bf16)
                part = _dot(vt_ref[hb // REP, :, rows], pt)
                acc = part if acc is None else acc + part
        if hb >= 0:
            o_ref[hb * HEAD_DIM:(hb + 1) * HEAD_DIM, :] = (
                acc[:HEAD_DIM] / acc[HEAD_DIM:HEAD_DIM + 1]).astype(bf16)
        if ha < ATT_HEADS:
            col_max[ha] = jnp.max(mx, axis=0, keepdims=True)


def _attention(qt, kk, vt):
    tq = ATT_TQ
    qt_per_seq = SEQ // tq
    gw = ATT_HEADS * HEAD_DIM
    return pl.pallas_call(
        _attn_kernel,
        out_shape=jax.ShapeDtypeStruct((D, N_LAT), bf16),
        grid=(BATCH, N_KV // ATT_GROUPS, qt_per_seq),
        in_specs=[
            pl.BlockSpec((gw, tq), lambda b, g, t: (g, b * qt_per_seq + t)),
            pl.BlockSpec((None, ATT_GROUPS, LK, LANES), lambda b, g, t: (b, g, 0, 0)),
            pl.BlockSpec((None, ATT_GROUPS, LANES, LK), lambda b, g, t: (b, g, 0, 0)),
        ],
        out_specs=pl.BlockSpec((gw, tq), lambda b, g, t: (g, b * qt_per_seq + t)),
        scratch_shapes=[pltpu.VMEM((LK, tq), f32)] * (ATT_LAG + 1),
        compiler_params=_params(("parallel", "parallel", "arbitrary")),
        name="attention",
    )(qt, kk, vt)


PROJ_TM = 512


N_ANCHORS = 3


def _anchor_specs():
    return [pl.BlockSpec(memory_space=pl.ANY)] * N_ANCHORS


def _oproj_kernel(o_ref, x_ref, gt_ref, w_ref, *rest):
    out_ref = rest[N_ANCHORS]
    out_ref[...] = x_ref[...] + gt_ref[...] * _dot(o_ref[...], w_ref[...])


def _attn_out(o, x_all, mods, w_o, anchors):
    tm = PROJ_TM
    return pl.pallas_call(
        _oproj_kernel,
        out_shape=jax.ShapeDtypeStruct((N_LAT, D), f32),
        grid=(N_LAT // tm,),
        in_specs=[
            pl.BlockSpec((tm, D), lambda t: (t, 0)),
            pl.BlockSpec((tm, D), lambda t: (t, 0)),
            _mod_spec(2, tm),
            _full_spec((D, D)),
        ] + _anchor_specs(),
        out_specs=pl.BlockSpec((tm, D), lambda t: (t, 0)),
        compiler_params=_params(("parallel",)),
        name="attn_out",
    )(o, x_all, mods, w_o, *anchors)


POOL_TM = 512
POOL_HALO = 8
POOL_EXT = POOL_TM + 2 * POOL_HALO
assert max(POOL_WINDOWS) // 2 <= POOL_HALO


def _pool_kernel(xp_ref, x_ref, xn_ref, sh_ref, sc_ref, gt_ref, g_ref, pw_ref, ps_ref, *rest):
    o_ref = rest[N_ANCHORS]
    t = pl.program_id(0)
    tiles_per_seq = SEQ // POOL_TM
    ts = t % tiles_per_seq
    x = x_ref[...]
    g, sh, sc = g_ref[...], sh_ref[...], sc_ref[...]
    h = _modulate(x, g, sh, sc)
    hp = jnp.where(ts > 0, _modulate(xp_ref[...], g, sh, sc), 0.0)
    hn = jnp.where(ts < tiles_per_seq - 1, _modulate(xn_ref[...], g, sh, sc), 0.0)
    ext = jnp.concatenate([hp, h, hn], axis=0)
    pos = ts * POOL_TM + lax.broadcasted_iota(jnp.int32, (POOL_TM, 1), 0)
    for j, w in enumerate(POOL_WINDOWS):
        c0 = j * POOL_GROUP
        p = ext[:, c0:c0 + POOL_GROUP]
        k = 1
        while k < w:
            p = p + pltpu.roll(p, POOL_EXT - k, 0)
            k *= 2
        first = POOL_HALO - w // 2
        s = (pltpu.roll(p, POOL_EXT - first, 0) if first else p)[:POOL_TM]
        lo_i = jnp.maximum(pos - w // 2, 0)
        hi_i = jnp.minimum(pos + w // 2 - 1, SEQ - 1)
        cnt = (hi_i - lo_i + 1).astype(f32)
        d = (s / cnt - h[:, c0:c0 + POOL_GROUP]).astype(bf16)
        y = _dot(d, pw_ref[j]) * ps_ref[:, c0:c0 + POOL_GROUP]
        o_ref[:, c0:c0 + POOL_GROUP] = x[:, c0:c0 + POOL_GROUP] + gt_ref[:, c0:c0 + POOL_GROUP] * y


def _pool_mixer(x, mods, g, p_w, p_scale, anchors):
    tm = POOL_TM
    hb = tm // POOL_HALO
    n_halo_blocks = N_LAT // POOL_HALO
    return pl.pallas_call(
        _pool_kernel,
        out_shape=jax.ShapeDtypeStruct((N_LAT, D), f32),
        grid=(N_LAT // tm,),
        in_specs=[
            pl.BlockSpec((POOL_HALO, D), lambda t: (jnp.maximum(t * hb - 1, 0), 0)),
            pl.BlockSpec((tm, D), lambda t: (t, 0)),
            pl.BlockSpec((POOL_HALO, D), lambda t: (jnp.minimum((t + 1) * hb, n_halo_blocks - 1), 0)),
            _mod_spec(0, tm), _mod_spec(1, tm), _mod_spec(2, tm),
            _full_spec((1, D)),
            _full_spec((len(POOL_WINDOWS), POOL_GROUP, POOL_GROUP)),
            _full_spec((1, D)),
        ] + _anchor_specs(),
        out_specs=pl.BlockSpec((tm, D), lambda t: (t, 0)),
        compiler_params=_params(("parallel",)),
        name="pool_mixer",
    )(x, x, x, mods, mods, mods, g, p_w, p_scale, *anchors)


ROUTER_TM = 512


def _router_kernel(x_ref, sh_ref, sc_ref, g_ref, wr_ref, ltri_ref, h_ref, route_ref, cnt_ref, run_ref):
    @pl.when(pl.program_id(0) == 0)
    def _():
        run_ref[...] = jnp.zeros_like(run_ref)

    h = _modulate(x_ref[...], g_ref[...], sh_ref[...], sc_ref[...])
    h_ref[...] = h
    h_hi = h.astype(bf16)
    h_lo = (h - h_hi.astype(f32)).astype(bf16)
    p1 = _dot(h_hi, wr_ref[0])
    logits = p1 + pltpu.roll(p1, LANES - N_EXPERTS, 1) + _dot(h_lo, wr_ref[1])
    lane = lax.broadcasted_iota(jnp.int32, logits.shape, 1)
    neg = jnp.float32(-jnp.inf)
    logits = jnp.where(lane < N_EXPERTS, logits, neg)
    m1 = jnp.max(logits, axis=-1, keepdims=True)
    i1 = jnp.min(jnp.where(logits == m1, lane, LANES), axis=-1, keepdims=True)
    rest = jnp.where(lane == i1, neg, logits)
    m2 = jnp.max(rest, axis=-1, keepdims=True)
    i2 = jnp.min(jnp.where(rest == m2, lane, LANES), axis=-1, keepdims=True)
    e2 = jnp.exp(m2 - m1)
    g1 = 1.0 / (1.0 + e2)
    g2 = e2 / (1.0 + e2)
    oh1 = jnp.where(lane == i1, 1.0, 0.0)
    oh2 = jnp.where(lane == i2, 1.0, 0.0)
    ltri = ltri_ref[...]
    before1 = _dot(ltri, oh1.astype(bf16))
    before2 = _dot(ltri, oh2.astype(bf16))
    tot1 = jnp.sum(oh1, axis=0, keepdims=True)
    tot2 = jnp.sum(oh2, axis=0, keepdims=True)
    run = run_ref[...]
    lp1 = jnp.sum(oh1 * (run + before1), axis=-1, keepdims=True)
    lp2 = jnp.sum(oh2 * (run + tot1 + before2), axis=-1, keepdims=True)
    run = run + tot1 + tot2
    run_ref[...] = run
    cnt_ref[...] = run
    route = jnp.where(lane == 0, i1.astype(f32),
                      jnp.where(lane == 1, i2.astype(f32),
                                jnp.where(lane == 2, g1,
                                          jnp.where(lane == 3, g2,
                                                    jnp.where(lane == 4, lp1, jnp.where(lane == 5, lp2, 0.0))))))
    route_ref[...] = route


def _router(x, mods, g, wr_pad):
    tm = ROUTER_TM
    ltri = jnp.asarray(np.tril(np.ones((tm, tm), np.float32), -1), dtype=bf16)
    return pl.pallas_call(
        _router_kernel,
        out_shape=(jax.ShapeDtypeStruct((N_LAT, D), f32),
                   jax.ShapeDtypeStruct((N_LAT, LANES), f32),
                   jax.ShapeDtypeStruct((1, LANES), f32)),
        grid=(N_LAT // tm,),
        in_specs=[
            pl.BlockSpec((tm, D), lambda t: (t, 0)),
            _mod_spec(3, tm), _mod_spec(4, tm),
            _full_spec((1, D)),
            _full_spec((2, D, LANES)),
            _full_spec((tm, tm)),
        ],
        out_specs=(pl.BlockSpec((tm, D), lambda t: (t, 0)),
                   pl.BlockSpec((tm, LANES), lambda t: (t, 0)),
                   _full_spec((1, LANES))),
        scratch_shapes=[pltpu.VMEM((1, LANES), f32)],
        compiler_params=_params(("arbitrary",)),
        name="router",
    )(x, mods, mods, g, wr_pad, ltri)


CAST_BLOCK_BYTES = 8 * 1024 * 1024


def _cast_kernel(w_ref, o_ref):
    o_ref[...] = w_ref[...].astype(bf16)


def _layer_weights_bf16(w, layer):
    n_layers, n_e, r, c = w.shape
    rows = n_e * r
    block_rows = 1 << ((CAST_BLOCK_BYTES // (4 * c)).bit_length() - 1)
    assert rows % block_rows == 0
    steps = rows // block_rows
    out = pl.pallas_call(
        _cast_kernel,
        out_shape=jax.ShapeDtypeStruct((rows, c), bf16),
        grid=(steps,),
        in_specs=[pl.BlockSpec((block_rows, c), lambda i: (layer * steps + i, 0))],
        out_specs=pl.BlockSpec((block_rows, c), lambda i: (i, 0)),
        compiler_params=_params(("parallel",)),
        name="cast_expert_weights",
    )(w.reshape(n_layers * rows, c))
    return out.reshape(n_e, r, c)


MOE_TM = 512
MOE_TF = 1792
MOE_TILES = (2 * N_LAT) // MOE_TM + N_EXPERTS
MOE_ROWS = MOE_TILES * MOE_TM
MOE_XS_PARTS = 3
MOE_PART_TILES = MOE_TILES // MOE_XS_PARTS
assert MOE_PART_TILES * MOE_XS_PARTS == MOE_TILES


def _expert_kernel(te_ref, nt_ref, *refs):
    xs_refs = refs[:MOE_XS_PARTS]
    wg_ref, wu_ref, wd_ref, y_ref, xb_ref, acc_ref = refs[MOE_XS_PARTS:]
    t = pl.program_id(0)
    j = pl.program_id(1)
    last = pl.num_programs(1) - 1
    active = t < nt_ref[0]

    @pl.when(active)
    def _():
        @pl.when(j == 0)
        def _():
            for k, xs_ref in enumerate(xs_refs):
                @pl.when(t // MOE_PART_TILES == k)
                def _():
                    xb_ref[...] = xs_ref[...].astype(bf16)
            acc_ref[...] = jnp.zeros_like(acc_ref)

        xs = xb_ref[...]
        a = _dot(xs, wg_ref[...])
        b = _dot(xs, wu_ref[...])
        act = (a * jax.nn.sigmoid(a) * b).astype(bf16)
        acc_ref[...] += _dot(act, wd_ref[...])

        @pl.when(j == last)
        def _():
            y_ref[...] = acc_ref[...]

    @pl.when(jnp.logical_not(active) & (j == last))
    def _():
        y_ref[...] = jnp.zeros_like(y_ref)


def _experts(tile_expert, num_tiles, xs_parts, wg, wu, wd):
    tm, tf = MOE_TM, MOE_TF
    nj = EXPERT_DIM // tf

    def xs_spec(k):
        def index(t, j, te, nt):
            tt = jnp.minimum(t, jnp.maximum(nt[0] - 1, 0))
            return (jnp.clip(tt - k * MOE_PART_TILES, 0, MOE_PART_TILES - 1), 0)
        return pl.BlockSpec((tm, D), index)

    def w_col(t, j, te, nt):
        jj = jnp.where(t < nt[0], j, nj - 1)
        return (te[t], 0, jj)

    def w_row(t, j, te, nt):
        jj = jnp.where(t < nt[0], j, nj - 1)
        return (te[t], jj, 0)

    return pl.pallas_call(
        _expert_kernel,
        out_shape=jax.ShapeDtypeStruct((MOE_ROWS, D), f32),
        grid_spec=pltpu.PrefetchScalarGridSpec(
            num_scalar_prefetch=2,
            grid=(MOE_TILES, nj),
            in_specs=[xs_spec(k) for k in range(MOE_XS_PARTS)] + [
                pl.BlockSpec((None, D, tf), w_col),
                pl.BlockSpec((None, D, tf), w_col),
                pl.BlockSpec((None, tf, D), w_row),
            ],
            out_specs=pl.BlockSpec((tm, D), lambda t, j, te, nt: (t, 0)),
            scratch_shapes=[pltpu.VMEM((tm, D), bf16), pltpu.VMEM((tm, D), f32)],
        ),
        compiler_params=_params(("arbitrary", "arbitrary")),
        name="moe_experts",
    )(tile_expert, num_tiles, *xs_parts, wg, wu, wd)


COMBINE_TM = 512


def _combine_kernel(x_ref, y1_ref, y2_ref, route_ref, gt_ref, o_ref):
    route = route_ref[...]
    mix = route[:, 2:3] * y1_ref[...] + route[:, 3:4] * y2_ref[...]
    o_ref[...] = x_ref[...] + gt_ref[...] * mix


def _combine(x, y1, y2, route, mods):
    tm = COMBINE_TM
    return pl.pallas_call(
        _combine_kernel,
        out_shape=jax.ShapeDtypeStruct((N_LAT, D), f32),
        grid=(N_LAT // tm,),
        in_specs=[
            pl.BlockSpec((tm, D), lambda t: (t, 0)),
            pl.BlockSpec((tm, D), lambda t: (t, 0)),
            pl.BlockSpec((tm, D), lambda t: (t, 0)),
            pl.BlockSpec((tm, LANES), lambda t: (t, 0)),
            _mod_spec(5, tm),
        ],
        out_specs=pl.BlockSpec((tm, D), lambda t: (t, 0)),
        compiler_params=_params(("parallel",)),
        name="moe_combine",
    )(x, y1, y2, route, mods)


def _expert_table(table, e):
    ids = jnp.arange(N_EXPERTS, dtype=jnp.int32)
    return jnp.sum(jnp.where(e[:, None] == ids[None, :], table[None, :], 0), axis=1)


def _take_rows(a, idx):
    return a.at[idx].get(mode="promise_in_bounds")


def _moe(x, mods, g, wr_pad, wg, wu, wd, expert_base):
    tm = MOE_TM
    hp, route, cnt = _router(x, mods, g, wr_pad)
    e1 = route[:, 0].astype(jnp.int32)
    e2 = route[:, 1].astype(jnp.int32)
    counts = cnt[0, :N_EXPERTS].astype(jnp.int32)
    tiles_per_e = (counts + tm - 1) // tm
    tile_end = jnp.cumsum(tiles_per_e)
    group_base = (tile_end - tiles_per_e) * tm
    dense_base = jnp.cumsum(counts) - counts
    pos1 = _expert_table(group_base, e1) + route[:, 4].astype(jnp.int32)
    pos2 = _expert_table(group_base, e2) + route[:, 5].astype(jnp.int32)
    num_tiles = tile_end[-1]
    tile_ids = jnp.arange(MOE_TILES, dtype=jnp.int32)
    tile_expert = jnp.sum(tile_end[None, :] <= jnp.minimum(tile_ids, num_tiles - 1)[:, None], axis=1)
    tile_expert = jnp.minimum(tile_expert, N_EXPERTS - 1).astype(jnp.int32)
    tok = jnp.arange(N_LAT, dtype=jnp.int32)
    _, sorted_tok = lax.sort_key_val(jnp.concatenate([pos1, pos2]), jnp.concatenate([tok, tok]))
    row_expert = jnp.repeat(tile_expert, tm)
    src = jnp.arange(MOE_ROWS, dtype=jnp.int32) - _expert_table(group_base - dense_base, row_expert)
    row_token = _take_rows(sorted_tok, jnp.clip(src, 0, 2 * N_LAT - 1))
    part_rows = MOE_PART_TILES * tm
    xs_parts = [_take_rows(hp, row_token[k * part_rows:(k + 1) * part_rows]) for k in range(MOE_XS_PARTS)]
    y = _experts(tile_expert + expert_base, num_tiles.reshape(1).astype(jnp.int32), xs_parts, wg, wu, wd)
    return _combine(x, _take_rows(y, pos1), _take_rows(y, pos2), route, mods)


def _rope_tables():
    rows = SEQ // GRID_W
    r = jnp.repeat(jnp.arange(rows), GRID_W)
    col = jnp.tile(jnp.arange(GRID_W), rows)
    inv = ROPE_THETA ** (-jnp.arange(ROPE_FREQS, dtype=f32) / ROPE_FREQS)
    ang = jnp.stack([r, col], axis=-1).astype(f32)[..., None] * inv
    c = jnp.cos(ang)
    s = jnp.sin(ang)
    cos_h = jnp.concatenate([c[:, 0], c[:, 0], c[:, 1], c[:, 1]], axis=-1)
    sin_h = jnp.concatenate([-s[:, 0], s[:, 0], -s[:, 1], s[:, 1]], axis=-1)
    cos_t = jnp.tile(cos_h, (1, HEADS_PER_VREG))
    sin_t = jnp.tile(sin_h, (1, HEADS_PER_VREG))
    cos_t = jnp.concatenate([cos_t, jnp.ones((QKV_TM, LANES), f32)], axis=0)
    sin_t = jnp.concatenate([sin_t, jnp.zeros((QKV_TM, LANES), f32)], axis=0)
    return cos_t, sin_t


def kernel(x, c, ctx, c_ctx, ada_w, ada_b, norm_g, a_w_in, a_v_g, a_ws, a_bs, a_w_out, b_w_qkv, b_q_g, b_k_g,
           b_w_o, p_w, p_scale, f_w_gate, f_w_up, f_w_down, m_router, m_w_gate, m_w_up, m_w_down):
    cond = jnp.concatenate([c, c_ctx[None, :], jnp.zeros((MOD_ROWS - BATCH - 1, D), f32)], axis=0)
    ada = _ada_all(cond, ada_w, ada_b)
    mods = ada.reshape(DEPTH, MOD_ROWS, 6, 1, D).transpose(0, 2, 1, 3, 4)

    gmlp_w_in = a_w_in.astype(bf16)
    gmlp_w_out = a_w_out.astype(bf16)

    def gmlp_weights(j):
        b_full = jnp.repeat(a_bs[j].T, A_GW, axis=1)
        return gmlp_w_in, a_v_g[j][None, :], a_ws[j].astype(bf16), b_full, gmlp_w_out, j

    def moe_weights(f):
        w_hi = m_router[f].astype(bf16)
        w_lo = (m_router[f] - w_hi.astype(f32)).astype(bf16)
        zeros = jnp.zeros((D, LANES - 2 * N_EXPERTS), bf16)
        wr = jnp.stack([jnp.concatenate([w_hi, w_lo, zeros], axis=1),
                        jnp.concatenate([w_hi, jnp.zeros_like(w_lo), zeros], axis=1)])
        return (wr, _layer_weights_bf16(m_w_gate, f), _layer_weights_bf16(m_w_up, f),
                _layer_weights_bf16(m_w_down, f), 0)

    x_lat = x.reshape(N_LAT, D)
    x_ctx = ctx.reshape(N_CTX, D)

    x_all = _gmlp_mixer(x_lat, x_ctx, True, mods[0], norm_g[0, 0][None, :], *gmlp_weights(0))
    ffn_w = (f_w_gate.astype(bf16), f_w_up.astype(bf16), f_w_down.astype(bf16))
    x_all = _dense_ffn(x_all, mods[0], norm_g[0, 1][None, :], *ffn_w, 0)

    gain = jnp.concatenate([jnp.tile(b_q_g[0], N_HEADS), jnp.tile(b_k_g[0], N_KV)])[None, :]
    head_of_lane = np.arange(LANES) // HEAD_DIM
    bd = jnp.asarray((head_of_lane[:, None] == head_of_lane[None, :]).astype(np.float32), dtype=bf16)
    cos_t, sin_t = _rope_tables()
    qkv = _qkv_proj(x_all, mods[1], norm_g[1, 0][None, :], b_w_qkv[0].astype(bf16), gain, bd, cos_t, sin_t)

    def keys_first(a):
        return jnp.concatenate([a[N_LAT:].reshape(BATCH, CTX_LEN, -1), a[:N_LAT].reshape(BATCH, SEQ, -1)], axis=1)

    k = keys_first(qkv[:, N_HEADS * HEAD_DIM:QK_COLS]).reshape(BATCH, LK, N_KV, HEAD_DIM)
    v = keys_first(qkv[:, QK_COLS:]).reshape(BATCH, LK, N_KV, HEAD_DIM)
    kh = k.transpose(0, 2, 1, 3)
    kk = jnp.concatenate([kh] * HEADS_PER_VREG, axis=3)
    vt = v.transpose(0, 2, 3, 1)
    vt = jnp.concatenate([vt, jnp.ones((BATCH, N_KV, LANES - HEAD_DIM, LK), bf16)], axis=2)
    qt = qkv[:N_LAT, :N_HEADS * HEAD_DIM].T
    o = _attention(qt, kk, vt).T
    moe_w1 = moe_weights(0)
    xl = _attn_out(o, x_all, mods[1], b_w_o[0].astype(bf16), moe_w1[1:4])
    xl = _moe(xl, mods[1], norm_g[1, 1][None, :], *moe_w1)

    moe_w3 = moe_weights(1)
    xl = _pool_mixer(xl, mods[2], norm_g[2, 0][None, :], p_w[0].astype(bf16), p_scale[0][None, :], moe_w3[1:4])
    xl = _dense_ffn(xl, mods[2], norm_g[2, 1][None, :], *ffn_w, 1)

    xl = _gmlp_mixer(xl, x_ctx, False, mods[3], norm_g[3, 0][None, :], *gmlp_weights(1))
    xl = _moe(xl, mods[3], norm_g[3, 1][None, :], *moe_w3)
    return xl.reshape(BATCH, SEQ, D)
```

```python
import functools

import jax
import jax.numpy as jnp
import numpy as np
from jax import lax
from jax.experimental import pallas as pl
from jax.experimental.pallas import tpu as pltpu

D = 1024
BATCH = 4
SEQ = 4096
DEPTH = 4
GRID_W = 64
CTX_LEN = 256
CHUNK = 128
A_DIM = 2 * D
A_GROUPS = 8
A_GW = A_DIM // A_GROUPS
N_HEADS = 16
N_KV = 4
HEAD_DIM = 64
ROPE_FREQS = HEAD_DIM // 4
ROPE_THETA = 10000.0
POOL_WINDOWS = (2, 4, 8, 16)
POOL_GROUP = D // 4
FFN_DIM = 2816
N_EXPERTS = 8
EXPERT_DIM = 3584
EPS = 1e-6

N_LAT = BATCH * SEQ
N_CTX = BATCH * CTX_LEN
N_ALL = N_LAT + N_CTX
LK = CTX_LEN + SEQ
MOD_ROWS = 8
LANES = 128
QK_COLS = (N_HEADS + N_KV) * HEAD_DIM
QKV_COLS = (N_HEADS + 2 * N_KV) * HEAD_DIM

VMEM_LIMIT = 56 * 1024 * 1024

bf16 = jnp.bfloat16
f32 = jnp.float32


def _params(sem, vmem=VMEM_LIMIT):
    return pltpu.CompilerParams(dimension_semantics=sem, vmem_limit_bytes=vmem)


def _mod_row(t, tm):
    start = t * tm
    return jnp.where(start < N_LAT, start // SEQ, BATCH)


def _mod_spec(chunk, tm):
    return pl.BlockSpec((None, None, 1, D), lambda t, *_: (chunk, _mod_row(t, tm), 0, 0))


def _full_spec(shape):
    nd = len(shape)
    return pl.BlockSpec(shape, lambda *_: (0,) * nd)


def _modulate(x, g, shift, scale):
    ms = jnp.mean(x * x, axis=-1, keepdims=True)
    return (x * lax.rsqrt(ms + EPS) * g) * (1.0 + scale) + shift


LOG2E = 1.4426950408889634


def _gelu_tanh(z):
    k0 = -2.0 * 0.7978845608028654 * LOG2E
    return z / (1.0 + jnp.exp2(z * (z * z * (k0 * 0.044715) + k0)))


def _dot(a, b):
    return jnp.dot(a, b, preferred_element_type=f32)


ADA_TN = 1536


def _ada_kernel(cond_ref, w_ref, b_ref, o_ref):
    cond = cond_ref[...]
    s = (cond * jax.nn.sigmoid(cond)).astype(bf16)
    o_ref[...] = _dot(s, w_ref[...].astype(bf16)) + b_ref[...]


def _ada_all(cond, ada_w, ada_b):
    return pl.pallas_call(
        _ada_kernel,
        out_shape=jax.ShapeDtypeStruct((DEPTH, MOD_ROWS, 6 * D), f32),
        grid=(DEPTH, 6 * D // ADA_TN),
        in_specs=[
            _full_spec((MOD_ROWS, D)),
            pl.BlockSpec((None, D, ADA_TN), lambda i, n: (i, 0, n)),
            pl.BlockSpec((None, 1, ADA_TN), lambda i, n: (i, 0, n)),
        ],
        out_specs=pl.BlockSpec((None, MOD_ROWS, ADA_TN), lambda i, n: (i, 0, n)),
        compiler_params=_params(("parallel", "parallel")),
        name="ada",
    )(cond, ada_w, ada_b.reshape(DEPTH, 1, 6 * D))


GMLP_TM = 512


def _gmlp_kernel(x_ref, xc_ref, sh_ref, sc_ref, gt_ref, g_ref, win_ref, vg_ref, ws_ref, bs_ref, wout_ref,
                 o_ref, uv_ref):
    x = jnp.where(pl.program_id(0) < N_LAT // GMLP_TM, x_ref[...], xc_ref[...])
    h = _modulate(x, g_ref[...], sh_ref[...], sc_ref[...]).astype(bf16)
    z = _gelu_tanh(_dot(h, win_ref[...]))
    u = z[:, :A_DIM]
    v = z[:, A_DIM:]
    v = (v * lax.rsqrt(jnp.mean(v * v, axis=-1, keepdims=True) + EPS) * vg_ref[...]).astype(bf16)
    for c in range(GMLP_TM // CHUNK):
        r0 = c * CHUNK
        for g in range(A_GROUPS):
            c0 = g * A_GW
            mixed = _dot(ws_ref[g], v[r0:r0 + CHUNK, c0:c0 + A_GW]) + bs_ref[:, c0:c0 + A_GW]
            uv_ref[r0:r0 + CHUNK, c0:c0 + A_GW] = (u[r0:r0 + CHUNK, c0:c0 + A_GW] * mixed).astype(bf16)
    y = _dot(uv_ref[...], wout_ref[...])
    o_ref[...] = x + gt_ref[...] * y


def _gmlp_mixer(x, xc, with_ctx, mods, g, w_in, v_g, w_s, b_full, w_out, layer):
    rows = N_ALL if with_ctx else N_LAT
    tm = GMLP_TM
    lat_tiles = N_LAT // tm
    return pl.pallas_call(
        _gmlp_kernel,
        out_shape=jax.ShapeDtypeStruct((rows, D), f32),
        grid=(rows // tm,),
        in_specs=[
            pl.BlockSpec((tm, D), lambda t: (jnp.minimum(t, lat_tiles - 1), 0)),
            pl.BlockSpec((tm, D), lambda t: (jnp.maximum(t - lat_tiles, 0), 0)),
            _mod_spec(0, tm), _mod_spec(1, tm), _mod_spec(2, tm),
            _full_spec((1, D)),
            _resident_spec((D, 2 * A_DIM), layer),
            _full_spec((1, A_DIM)),
            _full_spec((A_GROUPS, CHUNK, CHUNK)),
            _full_spec((CHUNK, A_DIM)),
            _resident_spec((A_DIM, D), layer),
        ],
        out_specs=pl.BlockSpec((tm, D), lambda t: (t, 0)),
        scratch_shapes=[pltpu.VMEM((tm, A_DIM), bf16)],
        compiler_params=_params(("parallel",)),
        name="gmlp_mixer",
    )(x, xc, mods, mods, mods, g, w_in, v_g, w_s, b_full, w_out)


FFN_TM = 512


def _ffn_kernel(x_ref, sh_ref, sc_ref, gt_ref, g_ref, wg_ref, wu_ref, wd_ref, o_ref):
    x = x_ref[...]
    h = _modulate(x, g_ref[...], sh_ref[...], sc_ref[...]).astype(bf16)
    a = _dot(h, wg_ref[...])
    b = _dot(h, wu_ref[...])
    act = (a * jax.nn.sigmoid(a) * b).astype(bf16)
    o_ref[...] = x + gt_ref[...] * _dot(act, wd_ref[...])


def _resident_spec(shape, layer):
    nd = len(shape)
    return pl.BlockSpec((None,) + shape, lambda *_: (layer,) + (0,) * nd, pipeline_mode=pl.Buffered(1))


def _dense_ffn(x, mods, g, wg, wu, wd, layer):
    rows = x.shape[0]
    tm = FFN_TM
    return pl.pallas_call(
        _ffn_kernel,
        out_shape=jax.ShapeDtypeStruct((rows, D), f32),
        grid=(rows // tm,),
        in_specs=[
            pl.BlockSpec((tm, D), lambda t: (t, 0)),
            _mod_spec(3, tm), _mod_spec(4, tm), _mod_spec(5, tm),
            _full_spec((1, D)),
            _resident_spec((D, FFN_DIM), layer),
            _resident_spec((D, FFN_DIM), layer),
            _resident_spec((FFN_DIM, D), layer),
        ],
        out_specs=pl.BlockSpec((tm, D), lambda t: (t, 0)),
        compiler_params=_params(("parallel",)),
        name="dense_ffn",
    )(x, mods, mods, mods, g, wg, wu, wd)


QKV_TM = 512
HEADS_PER_VREG = LANES // HEAD_DIM
Q_SCALE = HEAD_DIM ** -0.5 * LOG2E


def _qkv_kernel(x_ref, sh_ref, sc_ref, g_ref, w_ref, gain_ref, bd_ref, cos_ref, sin_ref, o_ref):
    h = _modulate(x_ref[...], g_ref[...], sh_ref[...], sc_ref[...]).astype(bf16)
    qkv = _dot(h, w_ref[...])
    lane = lax.broadcasted_iota(jnp.int32, (QKV_TM, LANES), 1)
    first_half = (lane % (2 * ROPE_FREQS)) < ROPE_FREQS
    cos = cos_ref[...]
    sin = sin_ref[...]
    bd = bd_ref[...]
    for cb in range(QKV_COLS // LANES):
        c0 = cb * LANES
        blk = qkv[:, c0:c0 + LANES]
        if c0 < QK_COLS:
            sq = blk * blk
            hi = sq.astype(bf16)
            lo = (sq - hi.astype(f32)).astype(bf16)
            ms = (_dot(hi, bd) + _dot(lo, bd)) * (1.0 / HEAD_DIM)
            y = blk * lax.rsqrt(ms + EPS) * gain_ref[:, c0:c0 + LANES]
            partner = jnp.where(first_half, pltpu.roll(y, LANES - ROPE_FREQS, 1),
                                pltpu.roll(y, ROPE_FREQS, 1))
            y = y * cos + partner * sin
            if c0 < N_HEADS * HEAD_DIM:
                y = y * Q_SCALE
            blk = y
        o_ref[:, c0:c0 + LANES] = blk.astype(bf16)


def _qkv_proj(x_all, mods, g, w_qkv, gain, bd, cos_t, sin_t):
    tm = QKV_TM
    tiles_per_seq = SEQ // tm

    def tab_idx(t):
        return (jnp.where(t * tm < N_LAT, t % tiles_per_seq, tiles_per_seq), 0)

    return pl.pallas_call(
        _qkv_kernel,
        out_shape=jax.ShapeDtypeStruct((N_ALL, QKV_COLS), bf16),
        grid=(N_ALL // tm,),
        in_specs=[
            pl.BlockSpec((tm, D), lambda t: (t, 0)),
            _mod_spec(0, tm), _mod_spec(1, tm),
            _full_spec((1, D)),
            _full_spec((D, QKV_COLS)),
            _full_spec((1, QK_COLS)),
            _full_spec((LANES, LANES)),
            pl.BlockSpec((tm, LANES), tab_idx),
            pl.BlockSpec((tm, LANES), tab_idx),
        ],
        out_specs=pl.BlockSpec((tm, QKV_COLS), lambda t: (t, 0)),
        compiler_params=_params(("parallel",)),
        name="qkv_proj",
    )(x_all, mods, mods, g, w_qkv, gain, bd, cos_t, sin_t)


ATT_TQ = 512
REP = N_HEADS // N_KV
ATT_GROUPS = 2
ATT_HEADS = ATT_GROUPS * REP
ATT_KC = 256
ATT_LAG = 2


def _attn_kernel(qt_ref, kk_ref, vt_ref, o_ref, *st_refs):
    row = lax.broadcasted_iota(jnp.int32, (LANES, ATT_TQ), 0)
    low = row < HEAD_DIM
    zero = jnp.zeros((LANES, ATT_TQ), bf16)
    n_chunks = LK // ATT_KC
    sub = 8
    n_buf = len(st_refs)

    def masked_q(h):
        q2 = qt_ref[(h // HEADS_PER_VREG) * LANES:(h // HEADS_PER_VREG + 1) * LANES, :]
        return jnp.where(low if h % HEADS_PER_VREG == 0 else ~low, q2, zero)

    col_max = {}
    for ph in range(ATT_HEADS + ATT_LAG):
        ha, hb = ph, ph - ATT_LAG
        qm = masked_q(ha) if ha < ATT_HEADS else None
        mx = None
        acc = None
        for c in range(n_chunks):
            rows = slice(c * ATT_KC, (c + 1) * ATT_KC)
            if ha < ATT_HEADS:
                st = _dot(kk_ref[ha // REP, rows, :], qm)
                st_refs[ha % n_buf][rows, :] = st
                cm = jnp.max(st.reshape(ATT_KC // sub, sub, ATT_TQ), axis=0)
                mx = cm if mx is None else jnp.maximum(mx, cm)
            if hb >= 0:
                pt = jnp.exp2(st_refs[hb % n_buf][rows, :] - col_max[hb]).astype(bf16)
                part = _dot(vt_ref[hb // REP, :, rows], pt)
                acc = part if acc is None else acc + part
        if hb >= 0:
            o_ref[hb * HEAD_DIM:(hb + 1) * HEAD_DIM, :] = (
                acc[:HEAD_DIM] / acc[HEAD_DIM:HEAD_DIM + 1]).astype(bf16)
        if ha < ATT_HEADS:
            col_max[ha] = jnp.max(mx, axis=0, keepdims=True)


def _attention(qt, kk, vt):
    tq = ATT_TQ
    qt_per_seq = SEQ // tq
    gw = ATT_HEADS * HEAD_DIM
    return pl.pallas_call(
        _attn_kernel,
        out_shape=jax.ShapeDtypeStruct((D, N_LAT), bf16),
        grid=(BATCH, N_KV // ATT_GROUPS, qt_per_seq),
        in_specs=[
            pl.BlockSpec((gw, tq), lambda b, g, t: (g, b * qt_per_seq + t)),
            pl.BlockSpec((None, ATT_GROUPS, LK, LANES), lambda b, g, t: (b, g, 0, 0)),
            pl.BlockSpec((None, ATT_GROUPS, LANES, LK), lambda b, g, t: (b, g, 0, 0)),
        ],
        out_specs=pl.BlockSpec((gw, tq), lambda b, g, t: (g, b * qt_per_seq + t)),
        scratch_shapes=[pltpu.VMEM((LK, tq), f32)] * (ATT_LAG + 1),
        compiler_params=_params(("parallel", "parallel", "arbitrary")),
        name="attention",
    )(qt, kk, vt)


PROJ_TM = 512


N_ANCHORS = 3


def _anchor_specs():
    return [pl.BlockSpec(memory_space=pl.ANY)] * N_ANCHORS


def _oproj_kernel(o_ref, x_ref, gt_ref, w_ref, *rest):
    out_ref = rest[N_ANCHORS]
    out_ref[...] = x_ref[...] + gt_ref[...] * _dot(o_ref[...], w_ref[...])


def _attn_out(o, x_all, mods, w_o, anchors):
    tm = PROJ_TM
    return pl.pallas_call(
        _oproj_kernel,
        out_shape=jax.ShapeDtypeStruct((N_LAT, D), f32),
        grid=(N_LAT // tm,),
        in_specs=[
            pl.BlockSpec((tm, D), lambda t: (t, 0)),
            pl.BlockSpec((tm, D), lambda t: (t, 0)),
            _mod_spec(2, tm),
            _full_spec((D, D)),
        ] + _anchor_specs(),
        out_specs=pl.BlockSpec((tm, D), lambda t: (t, 0)),
        compiler_params=_params(("parallel",)),
        name="attn_out",
    )(o, x_all, mods, w_o, *anchors)


POOL_TM = 512
POOL_HALO = 8
POOL_EXT = POOL_TM + 2 * POOL_HALO
assert max(POOL_WINDOWS) // 2 <= POOL_HALO


def _pool_kernel(xp_ref, x_ref, xn_ref, sh_ref, sc_ref, gt_ref, g_ref, pw_ref, ps_ref, o_ref):
    t = pl.program_id(0)
    tiles_per_seq = SEQ // POOL_TM
    ts = t % tiles_per_seq
    x = x_ref[...]
    g, sh, sc = g_ref[...], sh_ref[...], sc_ref[...]
    h = _modulate(x, g, sh, sc)
    hp = jnp.where(ts > 0, _modulate(xp_ref[...], g, sh, sc), 0.0)
    hn = jnp.where(ts < tiles_per_seq - 1, _modulate(xn_ref[...], g, sh, sc), 0.0)
    ext = jnp.concatenate([hp, h, hn], axis=0)
    pos = ts * POOL_TM + lax.broadcasted_iota(jnp.int32, (POOL_TM, 1), 0)
    for j, w in enumerate(POOL_WINDOWS):
        c0 = j * POOL_GROUP
        p = ext[:, c0:c0 + POOL_GROUP]
        k = 1
        while k < w:
            p = p + pltpu.roll(p, POOL_EXT - k, 0)
            k *= 2
        first = POOL_HALO - w // 2
        s = (pltpu.roll(p, POOL_EXT - first, 0) if first else p)[:POOL_TM]
        lo_i = jnp.maximum(pos - w // 2, 0)
        hi_i = jnp.minimum(pos + w // 2 - 1, SEQ - 1)
        cnt = (hi_i - lo_i + 1).astype(f32)
        d = (s / cnt - h[:, c0:c0 + POOL_GROUP]).astype(bf16)
        y = _dot(d, pw_ref[j]) * ps_ref[:, c0:c0 + POOL_GROUP]
        o_ref[:, c0:c0 + POOL_GROUP] = x[:, c0:c0 + POOL_GROUP] + gt_ref[:, c0:c0 + POOL_GROUP] * y


def _pool_mixer(x, mods, g, p_w, p_scale):
    tm = POOL_TM
    hb = tm // POOL_HALO
    n_halo_blocks = N_LAT // POOL_HALO
    return pl.pallas_call(
        _pool_kernel,
        out_shape=jax.ShapeDtypeStruct((N_LAT, D), f32),
        grid=(N_LAT // tm,),
        in_specs=[
            pl.BlockSpec((POOL_HALO, D), lambda t: (jnp.maximum(t * hb - 1, 0), 0)),
            pl.BlockSpec((tm, D), lambda t: (t, 0)),
            pl.BlockSpec((POOL_HALO, D), lambda t: (jnp.minimum((t + 1) * hb, n_halo_blocks - 1), 0)),
            _mod_spec(0, tm), _mod_spec(1, tm), _mod_spec(2, tm),
            _full_spec((1, D)),
            _full_spec((len(POOL_WINDOWS), POOL_GROUP, POOL_GROUP)),
            _full_spec((1, D)),
        ],
        out_specs=pl.BlockSpec((tm, D), lambda t: (t, 0)),
        compiler_params=_params(("parallel",)),
        name="pool_mixer",
    )(x, x, x, mods, mods, mods, g, p_w, p_scale)


ROUTER_TM = 512


def _router_kernel(x_ref, sh_ref, sc_ref, g_ref, wr_ref, ltri_ref, h_ref, route_ref, cnt_ref, run_ref):
    @pl.when(pl.program_id(0) == 0)
    def _():
        run_ref[...] = jnp.zeros_like(run_ref)

    h = _modulate(x_ref[...], g_ref[...], sh_ref[...], sc_ref[...])
    h_ref[...] = h
    h_hi = h.astype(bf16)
    h_lo = (h - h_hi.astype(f32)).astype(bf16)
    p1 = _dot(h_hi, wr_ref[0])
    logits = p1 + pltpu.roll(p1, LANES - N_EXPERTS, 1) + _dot(h_lo, wr_ref[1])
    lane = lax.broadcasted_iota(jnp.int32, logits.shape, 1)
    neg = jnp.float32(-jnp.inf)
    logits = jnp.where(lane < N_EXPERTS, logits, neg)
    m1 = jnp.max(logits, axis=-1, keepdims=True)
    i1 = jnp.min(jnp.where(logits == m1, lane, LANES), axis=-1, keepdims=True)
    rest = jnp.where(lane == i1, neg, logits)
    m2 = jnp.max(rest, axis=-1, keepdims=True)
    i2 = jnp.min(jnp.where(rest == m2, lane, LANES), axis=-1, keepdims=True)
    e2 = jnp.exp(m2 - m1)
    g1 = 1.0 / (1.0 + e2)
    g2 = e2 / (1.0 + e2)
    oh1 = jnp.where(lane == i1, 1.0, 0.0)
    oh2 = jnp.where(lane == i2, 1.0, 0.0)
    ltri = ltri_ref[...]
    before1 = _dot(ltri, oh1.astype(bf16))
    before2 = _dot(ltri, oh2.astype(bf16))
    tot1 = jnp.sum(oh1, axis=0, keepdims=True)
    tot2 = jnp.sum(oh2, axis=0, keepdims=True)
    run = run_ref[...]
    lp1 = jnp.sum(oh1 * (run + before1), axis=-1, keepdims=True)
    lp2 = jnp.sum(oh2 * (run + tot1 + before2), axis=-1, keepdims=True)
    run = run + tot1 + tot2
    run_ref[...] = run
    cnt_ref[...] = run
    route = jnp.where(lane == 0, i1.astype(f32),
                      jnp.where(lane == 1, i2.astype(f32),
                                jnp.where(lane == 2, g1,
                                          jnp.where(lane == 3, g2,
                                                    jnp.where(lane == 4, lp1, jnp.where(lane == 5, lp2, 0.0))))))
    route_ref[...] = route


def _router(x, mods, g, wr_pad):
    tm = ROUTER_TM
    ltri = jnp.asarray(np.tril(np.ones((tm, tm), np.float32), -1), dtype=bf16)
    return pl.pallas_call(
        _router_kernel,
        out_shape=(jax.ShapeDtypeStruct((N_LAT, D), f32),
                   jax.ShapeDtypeStruct((N_LAT, LANES), f32),
                   jax.ShapeDtypeStruct((1, LANES), f32)),
        grid=(N_LAT // tm,),
        in_specs=[
            pl.BlockSpec((tm, D), lambda t: (t, 0)),
            _mod_spec(3, tm), _mod_spec(4, tm),
            _full_spec((1, D)),
            _full_spec((2, D, LANES)),
            _full_spec((tm, tm)),
        ],
        out_specs=(pl.BlockSpec((tm, D), lambda t: (t, 0)),
                   pl.BlockSpec((tm, LANES), lambda t: (t, 0)),
                   _full_spec((1, LANES))),
        scratch_shapes=[pltpu.VMEM((1, LANES), f32)],
        compiler_params=_params(("arbitrary",)),
        name="router",
    )(x, mods, mods, g, wr_pad, ltri)


MOE_TM = 512
MOE_TF = 1792
MOE_TILES = (2 * N_LAT) // MOE_TM + N_EXPERTS
MOE_ROWS = MOE_TILES * MOE_TM
MOE_XS_PARTS = 3
MOE_PART_TILES = MOE_TILES // MOE_XS_PARTS
assert MOE_PART_TILES * MOE_XS_PARTS == MOE_TILES


def _expert_kernel(te_ref, nt_ref, *refs):
    xs_refs = refs[:MOE_XS_PARTS]
    wg_ref, wu_ref, wd_ref, y_ref, xb_ref, acc_ref = refs[MOE_XS_PARTS:]
    t = pl.program_id(0)
    j = pl.program_id(1)
    last = pl.num_programs(1) - 1
    active = t < nt_ref[0]

    @pl.when(active)
    def _():
        @pl.when(j == 0)
        def _():
            for k, xs_ref in enumerate(xs_refs):
                @pl.when(t // MOE_PART_TILES == k)
                def _():
                    xb_ref[...] = xs_ref[...].astype(bf16)
            acc_ref[...] = jnp.zeros_like(acc_ref)

        xs = xb_ref[...]
        a = _dot(xs, wg_ref[...])
        b = _dot(xs, wu_ref[...])
        act = (a * jax.nn.sigmoid(a) * b).astype(bf16)
        acc_ref[...] += _dot(act, wd_ref[...])

        @pl.when(j == last)
        def _():
            y_ref[...] = acc_ref[...]

    @pl.when(jnp.logical_not(active) & (j == last))
    def _():
        y_ref[...] = jnp.zeros_like(y_ref)


def _experts(tile_expert, num_tiles, xs_parts, wg, wu, wd):
    tm, tf = MOE_TM, MOE_TF
    nj = EXPERT_DIM // tf

    def xs_spec(k):
        def index(t, j, te, nt):
            tt = jnp.minimum(t, jnp.maximum(nt[0] - 1, 0))
            return (jnp.clip(tt - k * MOE_PART_TILES, 0, MOE_PART_TILES - 1), 0)
        return pl.BlockSpec((tm, D), index)

    def w_col(t, j, te, nt):
        jj = jnp.where(t < nt[0], j, nj - 1)
        return (te[t], 0, jj)

    def w_row(t, j, te, nt):
        jj = jnp.where(t < nt[0], j, nj - 1)
        return (te[t], jj, 0)

    return pl.pallas_call(
        _expert_kernel,
        out_shape=jax.ShapeDtypeStruct((MOE_ROWS, D), f32),
        grid_spec=pltpu.PrefetchScalarGridSpec(
            num_scalar_prefetch=2,
            grid=(MOE_TILES, nj),
            in_specs=[xs_spec(k) for k in range(MOE_XS_PARTS)] + [
                pl.BlockSpec((None, D, tf), w_col),
                pl.BlockSpec((None, D, tf), w_col),
                pl.BlockSpec((None, tf, D), w_row),
            ],
            out_specs=pl.BlockSpec((tm, D), lambda t, j, te, nt: (t, 0)),
            scratch_shapes=[pltpu.VMEM((tm, D), bf16), pltpu.VMEM((tm, D), f32)],
        ),
        compiler_params=_params(("arbitrary", "arbitrary")),
        name="moe_experts",
    )(tile_expert, num_tiles, *xs_parts, wg, wu, wd)


COMBINE_TM = 512


def _combine_kernel(x_ref, y1_ref, y2_ref, route_ref, gt_ref, o_ref):
    route = route_ref[...]
    mix = route[:, 2:3] * y1_ref[...] + route[:, 3:4] * y2_ref[...]
    o_ref[...] = x_ref[...] + gt_ref[...] * mix


def _combine(x, y1, y2, route, mods):
    tm = COMBINE_TM
    return pl.pallas_call(
        _combine_kernel,
        out_shape=jax.ShapeDtypeStruct((N_LAT, D), f32),
        grid=(N_LAT // tm,),
        in_specs=[
            pl.BlockSpec((tm, D), lambda t: (t, 0)),
            pl.BlockSpec((tm, D), lambda t: (t, 0)),
            pl.BlockSpec((tm, D), lambda t: (t, 0)),
            pl.BlockSpec((tm, LANES), lambda t: (t, 0)),
            _mod_spec(5, tm),
        ],
        out_specs=pl.BlockSpec((tm, D), lambda t: (t, 0)),
        compiler_params=_params(("parallel",)),
        name="moe_combine",
    )(x, y1, y2, route, mods)


def _expert_table(table, e):
    ids = jnp.arange(N_EXPERTS, dtype=jnp.int32)
    return jnp.sum(jnp.where(e[:, None] == ids[None, :], table[None, :], 0), axis=1)


def _take_rows(a, idx):
    return a.at[idx].get(mode="promise_in_bounds")


def _moe(x, mods, g, wr_pad, wg, wu, wd, expert_base):
    tm = MOE_TM
    hp, route, cnt = _router(x, mods, g, wr_pad)
    e1 = route[:, 0].astype(jnp.int32)
    e2 = route[:, 1].astype(jnp.int32)
    counts = cnt[0, :N_EXPERTS].astype(jnp.int32)
    tiles_per_e = (counts + tm - 1) // tm
    tile_end = jnp.cumsum(tiles_per_e)
    group_base = (tile_end - tiles_per_e) * tm
    dense_base = jnp.cumsum(counts) - counts
    pos1 = _expert_table(group_base, e1) + route[:, 4].astype(jnp.int32)
    pos2 = _expert_table(group_base, e2) + route[:, 5].astype(jnp.int32)
    num_tiles = tile_end[-1]
    tile_ids = jnp.arange(MOE_TILES, dtype=jnp.int32)
    tile_expert = jnp.sum(tile_end[None, :] <= jnp.minimum(tile_ids, num_tiles - 1)[:, None], axis=1)
    tile_expert = jnp.minimum(tile_expert, N_EXPERTS - 1).astype(jnp.int32)
    tok = jnp.arange(N_LAT, dtype=jnp.int32)
    _, sorted_tok = lax.sort_key_val(jnp.concatenate([pos1, pos2]), jnp.concatenate([tok, tok]))
    row_expert = jnp.repeat(tile_expert, tm)
    src = jnp.arange(MOE_ROWS, dtype=jnp.int32) - _expert_table(group_base - dense_base, row_expert)
    row_token = _take_rows(sorted_tok, jnp.clip(src, 0, 2 * N_LAT - 1))
    part_rows = MOE_PART_TILES * tm
    xs_parts = [_take_rows(hp, row_token[k * part_rows:(k + 1) * part_rows]) for k in range(MOE_XS_PARTS)]
    y = _experts(tile_expert + expert_base, num_tiles.reshape(1).astype(jnp.int32), xs_parts, wg, wu, wd)
    return _combine(x, _take_rows(y, pos1), _take_rows(y, pos2), route, mods)


def _rope_tables():
    rows = SEQ // GRID_W
    r = jnp.repeat(jnp.arange(rows), GRID_W)
    col = jnp.tile(jnp.arange(GRID_W), rows)
    inv = ROPE_THETA ** (-jnp.arange(ROPE_FREQS, dtype=f32) / ROPE_FREQS)
    ang = jnp.stack([r, col], axis=-1).astype(f32)[..., None] * inv
    c = jnp.cos(ang)
    s = jnp.sin(ang)
    cos_h = jnp.concatenate([c[:, 0], c[:, 0], c[:, 1], c[:, 1]], axis=-1)
    sin_h = jnp.concatenate([-s[:, 0], s[:, 0], -s[:, 1], s[:, 1]], axis=-1)
    cos_t = jnp.tile(cos_h, (1, HEADS_PER_VREG))
    sin_t = jnp.tile(sin_h, (1, HEADS_PER_VREG))
    cos_t = jnp.concatenate([cos_t, jnp.ones((QKV_TM, LANES), f32)], axis=0)
    sin_t = jnp.concatenate([sin_t, jnp.zeros((QKV_TM, LANES), f32)], axis=0)
    return cos_t, sin_t


def kernel(x, c, ctx, c_ctx, ada_w, ada_b, norm_g, a_w_in, a_v_g, a_ws, a_bs, a_w_out, b_w_qkv, b_q_g, b_k_g,
           b_w_o, p_w, p_scale, f_w_gate, f_w_up, f_w_down, m_router, m_w_gate, m_w_up, m_w_down):
    cond = jnp.concatenate([c, c_ctx[None, :], jnp.zeros((MOD_ROWS - BATCH - 1, D), f32)], axis=0)
    ada = _ada_all(cond, ada_w, ada_b)
    mods = ada.reshape(DEPTH, MOD_ROWS, 6, 1, D).transpose(0, 2, 1, 3, 4)

    gmlp_w_in = a_w_in.astype(bf16)
    gmlp_w_out = a_w_out.astype(bf16)

    def gmlp_weights(j):
        b_full = jnp.repeat(a_bs[j].T, A_GW, axis=1)
        return gmlp_w_in, a_v_g[j][None, :], a_ws[j].astype(bf16), b_full, gmlp_w_out, j

    n_moe = m_w_gate.shape[0]
    moe_w = (m_w_gate.astype(bf16).reshape(n_moe * N_EXPERTS, D, EXPERT_DIM),
             m_w_up.astype(bf16).reshape(n_moe * N_EXPERTS, D, EXPERT_DIM),
             m_w_down.astype(bf16).reshape(n_moe * N_EXPERTS, EXPERT_DIM, D))

    def moe_weights(f):
        w_hi = m_router[f].astype(bf16)
        w_lo = (m_router[f] - w_hi.astype(f32)).astype(bf16)
        zeros = jnp.zeros((D, LANES - 2 * N_EXPERTS), bf16)
        wr = jnp.stack([jnp.concatenate([w_hi, w_lo, zeros], axis=1),
                        jnp.concatenate([w_hi, jnp.zeros_like(w_lo), zeros], axis=1)])
        return (wr, *moe_w, f * N_EXPERTS)

    x_lat = x.reshape(N_LAT, D)
    x_ctx = ctx.reshape(N_CTX, D)

    x_all = _gmlp_mixer(x_lat, x_ctx, True, mods[0], norm_g[0, 0][None, :], *gmlp_weights(0))
    ffn_w = (f_w_gate.astype(bf16), f_w_up.astype(bf16), f_w_down.astype(bf16))
    x_all = _dense_ffn(x_all, mods[0], norm_g[0, 1][None, :], *ffn_w, 0)

    gain = jnp.concatenate([jnp.tile(b_q_g[0], N_HEADS), jnp.tile(b_k_g[0], N_KV)])[None, :]
    head_of_lane = np.arange(LANES) // HEAD_DIM
    bd = jnp.asarray((head_of_lane[:, None] == head_of_lane[None, :]).astype(np.float32), dtype=bf16)
    cos_t, sin_t = _rope_tables()
    qkv = _qkv_proj(x_all, mods[1], norm_g[1, 0][None, :], b_w_qkv[0].astype(bf16), gain, bd, cos_t, sin_t)

    def keys_first(a):
        return jnp.concatenate([a[N_LAT:].reshape(BATCH, CTX_LEN, -1), a[:N_LAT].reshape(BATCH, SEQ, -1)], axis=1)

    k = keys_first(qkv[:, N_HEADS * HEAD_DIM:QK_COLS]).reshape(BATCH, LK, N_KV, HEAD_DIM)
    v = keys_first(qkv[:, QK_COLS:]).reshape(BATCH, LK, N_KV, HEAD_DIM)
    kh = k.transpose(0, 2, 1, 3)
    kk = jnp.concatenate([kh] * HEADS_PER_VREG, axis=3)
    vt = v.transpose(0, 2, 3, 1)
    vt = jnp.concatenate([vt, jnp.ones((BATCH, N_KV, LANES - HEAD_DIM, LK), bf16)], axis=2)
    qt = qkv[:N_LAT, :N_HEADS * HEAD_DIM].T
    o = _attention(qt, kk, vt).T
    xl = _attn_out(o, x_all, mods[1], b_w_o[0].astype(bf16), moe_w)
    xl = _moe(xl, mods[1], norm_g[1, 1][None, :], *moe_weights(0))

    xl = _pool_mixer(xl, mods[2], norm_g[2, 0][None, :], p_w[0].astype(bf16), p_scale[0][None, :])
    xl = _dense_ffn(xl, mods[2], norm_g[2, 1][None, :], *ffn_w, 1)

    xl = _gmlp_mixer(xl, x_ctx, False, mods[3], norm_g[3, 0][None, :], *gmlp_weights(1))
    xl = _moe(xl, mods[3], norm_g[3, 1][None, :], *moe_weights(1))
    return xl.reshape(BATCH, SEQ, D)
```

```python
import functools

import jax
import jax.numpy as jnp
import numpy as np
from jax import lax
from jax.experimental import pallas as pl
from jax.experimental.pallas import tpu as pltpu

D = 1024
BATCH = 4
SEQ = 4096
DEPTH = 4
GRID_W = 64
CTX_LEN = 256
CHUNK = 128
A_DIM = 2 * D
A_GROUPS = 8
A_GW = A_DIM // A_GROUPS
N_HEADS = 16
N_KV = 4
HEAD_DIM = 64
ROPE_FREQS = HEAD_DIM // 4
ROPE_THETA = 10000.0
POOL_WINDOWS = (2, 4, 8, 16)
POOL_GROUP = D // 4
FFN_DIM = 2816
N_EXPERTS = 8
EXPERT_DIM = 3584
EPS = 1e-6

N_LAT = BATCH * SEQ
N_CTX = BATCH * CTX_LEN
N_ALL = N_LAT + N_CTX
LK = CTX_LEN + SEQ
MOD_ROWS = 8
LANES = 128
QK_COLS = (N_HEADS + N_KV) * HEAD_DIM
QKV_COLS = (N_HEADS + 2 * N_KV) * HEAD_DIM

VMEM_LIMIT = 56 * 1024 * 1024

bf16 = jnp.bfloat16
f32 = jnp.float32


def _params(sem, vmem=VMEM_LIMIT):
    return pltpu.CompilerParams(dimension_semantics=sem, vmem_limit_bytes=vmem)


def _mod_row(t, tm):
    start = t * tm
    return jnp.where(start < N_LAT, start // SEQ, BATCH)


def _mod_spec(chunk, tm):
    return pl.BlockSpec((None, None, 1, D), lambda t, *_: (chunk, _mod_row(t, tm), 0, 0))


def _full_spec(shape):
    nd = len(shape)
    return pl.BlockSpec(shape, lambda *_: (0,) * nd)


def _modulate(x, g, shift, scale):
    ms = jnp.mean(x * x, axis=-1, keepdims=True)
    return (x * lax.rsqrt(ms + EPS) * g) * (1.0 + scale) + shift


LOG2E = 1.4426950408889634


def _gelu_tanh(z):
    k0 = -2.0 * 0.7978845608028654 * LOG2E
    return z / (1.0 + jnp.exp2(z * (z * z * (k0 * 0.044715) + k0)))


def _dot(a, b):
    return jnp.dot(a, b, preferred_element_type=f32)


ADA_TN = 1536


def _ada_kernel(cond_ref, w_ref, b_ref, o_ref):
    cond = cond_ref[...]
    s = (cond * jax.nn.sigmoid(cond)).astype(bf16)
    o_ref[...] = _dot(s, w_ref[...].astype(bf16)) + b_ref[...]


def _ada_all(cond, ada_w, ada_b):
    return pl.pallas_call(
        _ada_kernel,
        out_shape=jax.ShapeDtypeStruct((DEPTH, MOD_ROWS, 6 * D), f32),
        grid=(DEPTH, 6 * D // ADA_TN),
        in_specs=[
            _full_spec((MOD_ROWS, D)),
            pl.BlockSpec((None, D, ADA_TN), lambda i, n: (i, 0, n)),
            pl.BlockSpec((None, 1, ADA_TN), lambda i, n: (i, 0, n)),
        ],
        out_specs=pl.BlockSpec((None, MOD_ROWS, ADA_TN), lambda i, n: (i, 0, n)),
        compiler_params=_params(("parallel", "parallel")),
        name="ada",
    )(cond, ada_w, ada_b.reshape(DEPTH, 1, 6 * D))


GMLP_TM = 512


def _gmlp_kernel(x_ref, xc_ref, sh_ref, sc_ref, gt_ref, g_ref, win_ref, vg_ref, ws_ref, bs_ref, wout_ref,
                 o_ref, uv_ref):
    x = jnp.where(pl.program_id(0) < N_LAT // GMLP_TM, x_ref[...], xc_ref[...])
    h = _modulate(x, g_ref[...], sh_ref[...], sc_ref[...]).astype(bf16)
    z = _gelu_tanh(_dot(h, win_ref[...]))
    u = z[:, :A_DIM]
    v = z[:, A_DIM:]
    v = (v * lax.rsqrt(jnp.mean(v * v, axis=-1, keepdims=True) + EPS) * vg_ref[...]).astype(bf16)
    for c in range(GMLP_TM // CHUNK):
        r0 = c * CHUNK
        for g in range(A_GROUPS):
            c0 = g * A_GW
            mixed = _dot(ws_ref[g], v[r0:r0 + CHUNK, c0:c0 + A_GW]) + bs_ref[:, c0:c0 + A_GW]
            uv_ref[r0:r0 + CHUNK, c0:c0 + A_GW] = (u[r0:r0 + CHUNK, c0:c0 + A_GW] * mixed).astype(bf16)
    y = _dot(uv_ref[...], wout_ref[...])
    o_ref[...] = x + gt_ref[...] * y


def _gmlp_mixer(x, xc, with_ctx, mods, g, w_in, v_g, w_s, b_full, w_out, layer):
    rows = N_ALL if with_ctx else N_LAT
    tm = GMLP_TM
    lat_tiles = N_LAT // tm
    return pl.pallas_call(
        _gmlp_kernel,
        out_shape=jax.ShapeDtypeStruct((rows, D), f32),
        grid=(rows // tm,),
        in_specs=[
            pl.BlockSpec((tm, D), lambda t: (jnp.minimum(t, lat_tiles - 1), 0)),
            pl.BlockSpec((tm, D), lambda t: (jnp.maximum(t - lat_tiles, 0), 0)),
            _mod_spec(0, tm), _mod_spec(1, tm), _mod_spec(2, tm),
            _full_spec((1, D)),
            _resident_spec((D, 2 * A_DIM), layer),
            _full_spec((1, A_DIM)),
            _full_spec((A_GROUPS, CHUNK, CHUNK)),
            _full_spec((CHUNK, A_DIM)),
            _resident_spec((A_DIM, D), layer),
        ],
        out_specs=pl.BlockSpec((tm, D), lambda t: (t, 0)),
        scratch_shapes=[pltpu.VMEM((tm, A_DIM), bf16)],
        compiler_params=_params(("parallel",)),
        name="gmlp_mixer",
    )(x, xc, mods, mods, mods, g, w_in, v_g, w_s, b_full, w_out)


FFN_TM = 512


def _ffn_kernel(x_ref, sh_ref, sc_ref, gt_ref, g_ref, wg_ref, wu_ref, wd_ref, o_ref):
    x = x_ref[...]
    h = _modulate(x, g_ref[...], sh_ref[...], sc_ref[...]).astype(bf16)
    a = _dot(h, wg_ref[...])
    b = _dot(h, wu_ref[...])
    act = (a * jax.nn.sigmoid(a) * b).astype(bf16)
    o_ref[...] = x + gt_ref[...] * _dot(act, wd_ref[...])


def _resident_spec(shape, layer):
    nd = len(shape)
    return pl.BlockSpec((None,) + shape, lambda *_: (layer,) + (0,) * nd, pipeline_mode=pl.Buffered(1))


def _dense_ffn(x, mods, g, wg, wu, wd, layer):
    rows = x.shape[0]
    tm = FFN_TM
    return pl.pallas_call(
        _ffn_kernel,
        out_shape=jax.ShapeDtypeStruct((rows, D), f32),
        grid=(rows // tm,),
        in_specs=[
            pl.BlockSpec((tm, D), lambda t: (t, 0)),
            _mod_spec(3, tm), _mod_spec(4, tm), _mod_spec(5, tm),
            _full_spec((1, D)),
            _resident_spec((D, FFN_DIM), layer),
            _resident_spec((D, FFN_DIM), layer),
            _resident_spec((FFN_DIM, D), layer),
        ],
        out_specs=pl.BlockSpec((tm, D), lambda t: (t, 0)),
        compiler_params=_params(("parallel",)),
        name="dense_ffn",
    )(x, mods, mods, mods, g, wg, wu, wd)


QKV_TM = 512
HEADS_PER_VREG = LANES // HEAD_DIM
Q_SCALE = HEAD_DIM ** -0.5 * LOG2E


def _qkv_kernel(x_ref, sh_ref, sc_ref, g_ref, w_ref, gain_ref, bd_ref, cos_ref, sin_ref, o_ref):
    h = _modulate(x_ref[...], g_ref[...], sh_ref[...], sc_ref[...]).astype(bf16)
    qkv = _dot(h, w_ref[...])
    lane = lax.broadcasted_iota(jnp.int32, (QKV_TM, LANES), 1)
    first_half = (lane % (2 * ROPE_FREQS)) < ROPE_FREQS
    cos = cos_ref[...]
    sin = sin_ref[...]
    bd = bd_ref[...]
    for cb in range(QKV_COLS // LANES):
        c0 = cb * LANES
        blk = qkv[:, c0:c0 + LANES]
        if c0 < QK_COLS:
            sq = blk * blk
            hi = sq.astype(bf16)
            lo = (sq - hi.astype(f32)).astype(bf16)
            ms = (_dot(hi, bd) + _dot(lo, bd)) * (1.0 / HEAD_DIM)
            y = blk * lax.rsqrt(ms + EPS) * gain_ref[:, c0:c0 + LANES]
            partner = jnp.where(first_half, pltpu.roll(y, LANES - ROPE_FREQS, 1),
                                pltpu.roll(y, ROPE_FREQS, 1))
            y = y * cos + partner * sin
            if c0 < N_HEADS * HEAD_DIM:
                y = y * Q_SCALE
            blk = y
        o_ref[:, c0:c0 + LANES] = blk.astype(bf16)


def _qkv_proj(x_all, mods, g, w_qkv, gain, bd, cos_t, sin_t):
    tm = QKV_TM
    tiles_per_seq = SEQ // tm

    def tab_idx(t):
        return (jnp.where(t * tm < N_LAT, t % tiles_per_seq, tiles_per_seq), 0)

    return pl.pallas_call(
        _qkv_kernel,
        out_shape=jax.ShapeDtypeStruct((N_ALL, QKV_COLS), bf16),
        grid=(N_ALL // tm,),
        in_specs=[
            pl.BlockSpec((tm, D), lambda t: (t, 0)),
            _mod_spec(0, tm), _mod_spec(1, tm),
            _full_spec((1, D)),
            _full_spec((D, QKV_COLS)),
            _full_spec((1, QK_COLS)),
            _full_spec((LANES, LANES)),
            pl.BlockSpec((tm, LANES), tab_idx),
            pl.BlockSpec((tm, LANES), tab_idx),
        ],
        out_specs=pl.BlockSpec((tm, QKV_COLS), lambda t: (t, 0)),
        compiler_params=_params(("parallel",)),
        name="qkv_proj",
    )(x_all, mods, mods, g, w_qkv, gain, bd, cos_t, sin_t)


ATT_TQ = 512
REP = N_HEADS // N_KV
ATT_GROUPS = 2
ATT_HEADS = ATT_GROUPS * REP
ATT_KC = 256
ATT_LAG = 2


def _attn_kernel(qt_ref, kk_ref, vt_ref, o_ref, *st_refs):
    row = lax.broadcasted_iota(jnp.int32, (LANES, ATT_TQ), 0)
    low = row < HEAD_DIM
    zero = jnp.zeros((LANES, ATT_TQ), bf16)
    n_chunks = LK // ATT_KC
    sub = 8
    n_buf = len(st_refs)

    def masked_q(h):
        q2 = qt_ref[(h // HEADS_PER_VREG) * LANES:(h // HEADS_PER_VREG + 1) * LANES, :]
        return jnp.where(low if h % HEADS_PER_VREG == 0 else ~low, q2, zero)

    col_max = {}
    for ph in range(ATT_HEADS + ATT_LAG):
        ha, hb = ph, ph - ATT_LAG
        qm = masked_q(ha) if ha < ATT_HEADS else None
        mx = None
        acc = None
        for c in range(n_chunks):
            rows = slice(c * ATT_KC, (c + 1) * ATT_KC)
            if ha < ATT_HEADS:
                st = _dot(kk_ref[ha // REP, rows, :], qm)
                st_refs[ha % n_buf][rows, :] = st
                cm = jnp.max(st.reshape(ATT_KC // sub, sub, ATT_TQ), axis=0)
                mx = cm if mx is None else jnp.maximum(mx, cm)
            if hb >= 0:
                pt = jnp.exp2(st_refs[hb % n_buf][rows, :] - col_max[hb]).astype(bf16)
                part = _dot(vt_ref[hb // REP, :, rows], pt)
                acc = part if acc is None else acc + part
        if hb >= 0:
            o_ref[hb * HEAD_DIM:(hb + 1) * HEAD_DIM, :] = (
                acc[:HEAD_DIM] / acc[HEAD_DIM:HEAD_DIM + 1]).astype(bf16)
        if ha < ATT_HEADS:
            col_max[ha] = jnp.max(mx, axis=0, keepdims=True)


def _attention(qt, kk, vt):
    tq = ATT_TQ
    qt_per_seq = SEQ // tq
    gw = ATT_HEADS * HEAD_DIM
    return pl.pallas_call(
        _attn_kernel,
        out_shape=jax.ShapeDtypeStruct((D, N_LAT), bf16),
        grid=(BATCH, N_KV // ATT_GROUPS, qt_per_seq),
        in_specs=[
            pl.BlockSpec((gw, tq), lambda b, g, t: (g, b * qt_per_seq + t)),
            pl.BlockSpec((None, ATT_GROUPS, LK, LANES), lambda b, g, t: (b, g, 0, 0)),
            pl.BlockSpec((None, ATT_GROUPS, LANES, LK), lambda b, g, t: (b, g, 0, 0)),
        ],
        out_specs=pl.BlockSpec((gw, tq), lambda b, g, t: (g, b * qt_per_seq + t)),
        scratch_shapes=[pltpu.VMEM((LK, tq), f32)] * (ATT_LAG + 1),
        compiler_params=_params(("parallel", "parallel", "arbitrary")),
        name="attention",
    )(qt, kk, vt)


PROJ_TM = 512


def _oproj_kernel(o_ref, x_ref, gt_ref, w_ref, out_ref):
    out_ref[...] = x_ref[...] + gt_ref[...] * _dot(o_ref[...], w_ref[...])


def _attn_out(o, x_all, mods, w_o):
    tm = PROJ_TM
    return pl.pallas_call(
        _oproj_kernel,
        out_shape=jax.ShapeDtypeStruct((N_LAT, D), f32),
        grid=(N_LAT // tm,),
        in_specs=[
            pl.BlockSpec((tm, D), lambda t: (t, 0)),
            pl.BlockSpec((tm, D), lambda t: (t, 0)),
            _mod_spec(2, tm),
            _full_spec((D, D)),
        ],
        out_specs=pl.BlockSpec((tm, D), lambda t: (t, 0)),
        compiler_params=_params(("parallel",)),
        name="attn_out",
    )(o, x_all, mods, w_o)


POOL_TM = 512
POOL_HALO = 8
POOL_EXT = POOL_TM + 2 * POOL_HALO
assert max(POOL_WINDOWS) // 2 <= POOL_HALO


def _pool_kernel(xp_ref, x_ref, xn_ref, sh_ref, sc_ref, gt_ref, g_ref, pw_ref, ps_ref, o_ref):
    t = pl.program_id(0)
    tiles_per_seq = SEQ // POOL_TM
    ts = t % tiles_per_seq
    x = x_ref[...]
    g, sh, sc = g_ref[...], sh_ref[...], sc_ref[...]
    h = _modulate(x, g, sh, sc)
    hp = jnp.where(ts > 0, _modulate(xp_ref[...], g, sh, sc), 0.0)
    hn = jnp.where(ts < tiles_per_seq - 1, _modulate(xn_ref[...], g, sh, sc), 0.0)
    ext = jnp.concatenate([hp, h, hn], axis=0)
    pos = ts * POOL_TM + lax.broadcasted_iota(jnp.int32, (POOL_TM, 1), 0)
    for j, w in enumerate(POOL_WINDOWS):
        c0 = j * POOL_GROUP
        p = ext[:, c0:c0 + POOL_GROUP]
        k = 1
        while k < w:
            p = p + pltpu.roll(p, POOL_EXT - k, 0)
            k *= 2
        first = POOL_HALO - w // 2
        s = (pltpu.roll(p, POOL_EXT - first, 0) if first else p)[:POOL_TM]
        lo_i = jnp.maximum(pos - w // 2, 0)
        hi_i = jnp.minimum(pos + w // 2 - 1, SEQ - 1)
        cnt = (hi_i - lo_i + 1).astype(f32)
        d = (s / cnt - h[:, c0:c0 + POOL_GROUP]).astype(bf16)
        y = _dot(d, pw_ref[j]) * ps_ref[:, c0:c0 + POOL_GROUP]
        o_ref[:, c0:c0 + POOL_GROUP] = x[:, c0:c0 + POOL_GROUP] + gt_ref[:, c0:c0 + POOL_GROUP] * y


def _pool_mixer(x, mods, g, p_w, p_scale):
    tm = POOL_TM
    hb = tm // POOL_HALO
    n_halo_blocks = N_LAT // POOL_HALO
    return pl.pallas_call(
        _pool_kernel,
        out_shape=jax.ShapeDtypeStruct((N_LAT, D), f32),
        grid=(N_LAT // tm,),
        in_specs=[
            pl.BlockSpec((POOL_HALO, D), lambda t: (jnp.maximum(t * hb - 1, 0), 0)),
            pl.BlockSpec((tm, D), lambda t: (t, 0)),
            pl.BlockSpec((POOL_HALO, D), lambda t: (jnp.minimum((t + 1) * hb, n_halo_blocks - 1), 0)),
            _mod_spec(0, tm), _mod_spec(1, tm), _mod_spec(2, tm),
            _full_spec((1, D)),
            _full_spec((len(POOL_WINDOWS), POOL_GROUP, POOL_GROUP)),
            _full_spec((1, D)),
        ],
        out_specs=pl.BlockSpec((tm, D), lambda t: (t, 0)),
        compiler_params=_params(("parallel",)),
        name="pool_mixer",
    )(x, x, x, mods, mods, mods, g, p_w, p_scale)


ROUTER_TM = 512


def _router_kernel(x_ref, sh_ref, sc_ref, g_ref, wr_ref, ltri_ref, h_ref, route_ref, cnt_ref, run_ref):
    @pl.when(pl.program_id(0) == 0)
    def _():
        run_ref[...] = jnp.zeros_like(run_ref)

    h = _modulate(x_ref[...], g_ref[...], sh_ref[...], sc_ref[...])
    h_ref[...] = h
    h_hi = h.astype(bf16)
    h_lo = (h - h_hi.astype(f32)).astype(bf16)
    p1 = _dot(h_hi, wr_ref[0])
    logits = p1 + pltpu.roll(p1, LANES - N_EXPERTS, 1) + _dot(h_lo, wr_ref[1])
    lane = lax.broadcasted_iota(jnp.int32, logits.shape, 1)
    neg = jnp.float32(-jnp.inf)
    logits = jnp.where(lane < N_EXPERTS, logits, neg)
    m1 = jnp.max(logits, axis=-1, keepdims=True)
    i1 = jnp.min(jnp.where(logits == m1, lane, LANES), axis=-1, keepdims=True)
    rest = jnp.where(lane == i1, neg, logits)
    m2 = jnp.max(rest, axis=-1, keepdims=True)
    i2 = jnp.min(jnp.where(rest == m2, lane, LANES), axis=-1, keepdims=True)
    e2 = jnp.exp(m2 - m1)
    g1 = 1.0 / (1.0 + e2)
    g2 = e2 / (1.0 + e2)
    oh1 = jnp.where(lane == i1, 1.0, 0.0)
    oh2 = jnp.where(lane == i2, 1.0, 0.0)
    ltri = ltri_ref[...]
    before1 = _dot(ltri, oh1.astype(bf16))
    before2 = _dot(ltri, oh2.astype(bf16))
    tot1 = jnp.sum(oh1, axis=0, keepdims=True)
    tot2 = jnp.sum(oh2, axis=0, keepdims=True)
    run = run_ref[...]
    lp1 = jnp.sum(oh1 * (run + before1), axis=-1, keepdims=True)
    lp2 = jnp.sum(oh2 * (run + tot1 + before2), axis=-1, keepdims=True)
    run = run + tot1 + tot2
    run_ref[...] = run
    cnt_ref[...] = run
    route = jnp.where(lane == 0, i1.astype(f32),
                      jnp.where(lane == 1, i2.astype(f32),
                                jnp.where(lane == 2, g1,
                                          jnp.where(lane == 3, g2,
                                                    jnp.where(lane == 4, lp1, jnp.where(lane == 5, lp2, 0.0))))))
    route_ref[...] = route


def _router(x, mods, g, wr_pad):
    tm = ROUTER_TM
    ltri = jnp.asarray(np.tril(np.ones((tm, tm), np.float32), -1), dtype=bf16)
    return pl.pallas_call(
        _router_kernel,
        out_shape=(jax.ShapeDtypeStruct((N_LAT, D), f32),
                   jax.ShapeDtypeStruct((N_LAT, LANES), f32),
                   jax.ShapeDtypeStruct((1, LANES), f32)),
        grid=(N_LAT // tm,),
        in_specs=[
            pl.BlockSpec((tm, D), lambda t: (t, 0)),
            _mod_spec(3, tm), _mod_spec(4, tm),
            _full_spec((1, D)),
            _full_spec((2, D, LANES)),
            _full_spec((tm, tm)),
        ],
        out_specs=(pl.BlockSpec((tm, D), lambda t: (t, 0)),
                   pl.BlockSpec((tm, LANES), lambda t: (t, 0)),
                   _full_spec((1, LANES))),
        scratch_shapes=[pltpu.VMEM((1, LANES), f32)],
        compiler_params=_params(("arbitrary",)),
        name="router",
    )(x, mods, mods, g, wr_pad, ltri)


MOE_TM = 512
MOE_TF = 1792
MOE_TILES = (2 * N_LAT) // MOE_TM + N_EXPERTS
MOE_ROWS = MOE_TILES * MOE_TM
MOE_XS_PARTS = 3
MOE_PART_TILES = MOE_TILES // MOE_XS_PARTS
assert MOE_PART_TILES * MOE_XS_PARTS == MOE_TILES


def _expert_kernel(te_ref, nt_ref, *refs):
    xs_refs = refs[:MOE_XS_PARTS]
    wg_ref, wu_ref, wd_ref, y_ref, xb_ref, acc_ref = refs[MOE_XS_PARTS:]
    t = pl.program_id(0)
    j = pl.program_id(1)
    last = pl.num_programs(1) - 1
    active = t < nt_ref[0]

    @pl.when(active)
    def _():
        @pl.when(j == 0)
        def _():
            for k, xs_ref in enumerate(xs_refs):
                @pl.when(t // MOE_PART_TILES == k)
                def _():
                    xb_ref[...] = xs_ref[...].astype(bf16)
            acc_ref[...] = jnp.zeros_like(acc_ref)

        xs = xb_ref[...]
        a = _dot(xs, wg_ref[...])
        b = _dot(xs, wu_ref[...])
        act = (a * jax.nn.sigmoid(a) * b).astype(bf16)
        acc_ref[...] += _dot(act, wd_ref[...])

        @pl.when(j == last)
        def _():
            y_ref[...] = acc_ref[...]

    @pl.when(jnp.logical_not(active) & (j == last))
    def _():
        y_ref[...] = jnp.zeros_like(y_ref)


def _experts(tile_expert, num_tiles, xs_parts, wg, wu, wd):
    tm, tf = MOE_TM, MOE_TF
    nj = EXPERT_DIM // tf

    def xs_spec(k):
        def index(t, j, te, nt):
            tt = jnp.minimum(t, jnp.maximum(nt[0] - 1, 0))
            return (jnp.clip(tt - k * MOE_PART_TILES, 0, MOE_PART_TILES - 1), 0)
        return pl.BlockSpec((tm, D), index)

    def w_col(t, j, te, nt):
        jj = jnp.where(t < nt[0], j, nj - 1)
        return (te[t], 0, jj)

    def w_row(t, j, te, nt):
        jj = jnp.where(t < nt[0], j, nj - 1)
        return (te[t], jj, 0)

    return pl.pallas_call(
        _expert_kernel,
        out_shape=jax.ShapeDtypeStruct((MOE_ROWS, D), f32),
        grid_spec=pltpu.PrefetchScalarGridSpec(
            num_scalar_prefetch=2,
            grid=(MOE_TILES, nj),
            in_specs=[xs_spec(k) for k in range(MOE_XS_PARTS)] + [
                pl.BlockSpec((None, D, tf), w_col),
                pl.BlockSpec((None, D, tf), w_col),
                pl.BlockSpec((None, tf, D), w_row),
            ],
            out_specs=pl.BlockSpec((tm, D), lambda t, j, te, nt: (t, 0)),
            scratch_shapes=[pltpu.VMEM((tm, D), bf16), pltpu.VMEM((tm, D), f32)],
        ),
        compiler_params=_params(("arbitrary", "arbitrary")),
        name="moe_experts",
    )(tile_expert, num_tiles, *xs_parts, wg, wu, wd)


COMBINE_TM = 512


def _combine_kernel(x_ref, y1_ref, y2_ref, route_ref, gt_ref, o_ref):
    route = route_ref[...]
    mix = route[:, 2:3] * y1_ref[...] + route[:, 3:4] * y2_ref[...]
    o_ref[...] = x_ref[...] + gt_ref[...] * mix


def _combine(x, y1, y2, route, mods):
    tm = COMBINE_TM
    return pl.pallas_call(
        _combine_kernel,
        out_shape=jax.ShapeDtypeStruct((N_LAT, D), f32),
        grid=(N_LAT // tm,),
        in_specs=[
            pl.BlockSpec((tm, D), lambda t: (t, 0)),
            pl.BlockSpec((tm, D), lambda t: (t, 0)),
            pl.BlockSpec((tm, D), lambda t: (t, 0)),
            pl.BlockSpec((tm, LANES), lambda t: (t, 0)),
            _mod_spec(5, tm),
        ],
        out_specs=pl.BlockSpec((tm, D), lambda t: (t, 0)),
        compiler_params=_params(("parallel",)),
        name="moe_combine",
    )(x, y1, y2, route, mods)


def _expert_table(table, e):
    ids = jnp.arange(N_EXPERTS, dtype=jnp.int32)
    return jnp.sum(jnp.where(e[:, None] == ids[None, :], table[None, :], 0), axis=1)


def _take_rows(a, idx):
    return a.at[idx].get(mode="promise_in_bounds")


def _moe(x, mods, g, wr_pad, wg, wu, wd, expert_base):
    tm = MOE_TM
    hp, route, cnt = _router(x, mods, g, wr_pad)
    e1 = route[:, 0].astype(jnp.int32)
    e2 = route[:, 1].astype(jnp.int32)
    counts = cnt[0, :N_EXPERTS].astype(jnp.int32)
    tiles_per_e = (counts + tm - 1) // tm
    tile_end = jnp.cumsum(tiles_per_e)
    group_base = (tile_end - tiles_per_e) * tm
    dense_base = jnp.cumsum(counts) - counts
    pos1 = _expert_table(group_base, e1) + route[:, 4].astype(jnp.int32)
    pos2 = _expert_table(group_base, e2) + route[:, 5].astype(jnp.int32)
    num_tiles = tile_end[-1]
    tile_ids = jnp.arange(MOE_TILES, dtype=jnp.int32)
    tile_expert = jnp.sum(tile_end[None, :] <= jnp.minimum(tile_ids, num_tiles - 1)[:, None], axis=1)
    tile_expert = jnp.minimum(tile_expert, N_EXPERTS - 1).astype(jnp.int32)
    tok = jnp.arange(N_LAT, dtype=jnp.int32)
    _, sorted_tok = lax.sort_key_val(jnp.concatenate([pos1, pos2]), jnp.concatenate([tok, tok]))
    row_expert = jnp.repeat(tile_expert, tm)
    src = jnp.arange(MOE_ROWS, dtype=jnp.int32) - _expert_table(group_base - dense_base, row_expert)
    row_token = _take_rows(sorted_tok, jnp.clip(src, 0, 2 * N_LAT - 1))
    part_rows = MOE_PART_TILES * tm
    xs_parts = [_take_rows(hp, row_token[k * part_rows:(k + 1) * part_rows]) for k in range(MOE_XS_PARTS)]
    y = _experts(tile_expert + expert_base, num_tiles.reshape(1).astype(jnp.int32), xs_parts, wg, wu, wd)
    return _combine(x, _take_rows(y, pos1), _take_rows(y, pos2), route, mods)


def _rope_tables():
    rows = SEQ // GRID_W
    r = jnp.repeat(jnp.arange(rows), GRID_W)
    col = jnp.tile(jnp.arange(GRID_W), rows)
    inv = ROPE_THETA ** (-jnp.arange(ROPE_FREQS, dtype=f32) / ROPE_FREQS)
    ang = jnp.stack([r, col], axis=-1).astype(f32)[..., None] * inv
    c = jnp.cos(ang)
    s = jnp.sin(ang)
    cos_h = jnp.concatenate([c[:, 0], c[:, 0], c[:, 1], c[:, 1]], axis=-1)
    sin_h = jnp.concatenate([-s[:, 0], s[:, 0], -s[:, 1], s[:, 1]], axis=-1)
    cos_t = jnp.tile(cos_h, (1, HEADS_PER_VREG))
    sin_t = jnp.tile(sin_h, (1, HEADS_PER_VREG))
    cos_t = jnp.concatenate([cos_t, jnp.ones((QKV_TM, LANES), f32)], axis=0)
    sin_t = jnp.concatenate([sin_t, jnp.zeros((QKV_TM, LANES), f32)], axis=0)
    return cos_t, sin_t


def kernel(x, c, ctx, c_ctx, ada_w, ada_b, norm_g, a_w_in, a_v_g, a_ws, a_bs, a_w_out, b_w_qkv, b_q_g, b_k_g,
           b_w_o, p_w, p_scale, f_w_gate, f_w_up, f_w_down, m_router, m_w_gate, m_w_up, m_w_down):
    cond = jnp.concatenate([c, c_ctx[None, :], jnp.zeros((MOD_ROWS - BATCH - 1, D), f32)], axis=0)
    ada = _ada_all(cond, ada_w, ada_b)
    mods = ada.reshape(DEPTH, MOD_ROWS, 6, 1, D).transpose(0, 2, 1, 3, 4)

    gmlp_w_in = a_w_in.astype(bf16)
    gmlp_w_out = a_w_out.astype(bf16)

    def gmlp_weights(j):
        b_full = jnp.repeat(a_bs[j].T, A_GW, axis=1)
        return gmlp_w_in, a_v_g[j][None, :], a_ws[j].astype(bf16), b_full, gmlp_w_out, j

    n_moe = m_w_gate.shape[0]
    moe_w = (m_w_gate.astype(bf16).reshape(n_moe * N_EXPERTS, D, EXPERT_DIM),
             m_w_up.astype(bf16).reshape(n_moe * N_EXPERTS, D, EXPERT_DIM),
             m_w_down.astype(bf16).reshape(n_moe * N_EXPERTS, EXPERT_DIM, D))

    def moe_weights(f):
        w_hi = m_router[f].astype(bf16)
        w_lo = (m_router[f] - w_hi.astype(f32)).astype(bf16)
        zeros = jnp.zeros((D, LANES - 2 * N_EXPERTS), bf16)
        wr = jnp.stack([jnp.concatenate([w_hi, w_lo, zeros], axis=1),
                        jnp.concatenate([w_hi, jnp.zeros_like(w_lo), zeros], axis=1)])
        return (wr, *moe_w, f * N_EXPERTS)

    x_lat = x.reshape(N_LAT, D)
    x_ctx = ctx.reshape(N_CTX, D)

    x_all = _gmlp_mixer(x_lat, x_ctx, True, mods[0], norm_g[0, 0][None, :], *gmlp_weights(0))
    ffn_w = (f_w_gate.astype(bf16), f_w_up.astype(bf16), f_w_down.astype(bf16))
    x_all = _dense_ffn(x_all, mods[0], norm_g[0, 1][None, :], *ffn_w, 0)

    gain = jnp.concatenate([jnp.tile(b_q_g[0], N_HEADS), jnp.tile(b_k_g[0], N_KV)])[None, :]
    head_of_lane = np.arange(LANES) // HEAD_DIM
    bd = jnp.asarray((head_of_lane[:, None] == head_of_lane[None, :]).astype(np.float32), dtype=bf16)
    cos_t, sin_t = _rope_tables()
    qkv = _qkv_proj(x_all, mods[1], norm_g[1, 0][None, :], b_w_qkv[0].astype(bf16), gain, bd, cos_t, sin_t)

    def keys_first(a):
        return jnp.concatenate([a[N_LAT:].reshape(BATCH, CTX_LEN, -1), a[:N_LAT].reshape(BATCH, SEQ, -1)], axis=1)

    k = keys_first(qkv[:, N_HEADS * HEAD_DIM:QK_COLS]).reshape(BATCH, LK, N_KV, HEAD_DIM)
    v = keys_first(qkv[:, QK_COLS:]).reshape(BATCH, LK, N_KV, HEAD_DIM)
    kh = k.transpose(0, 2, 1, 3)
    kk = jnp.concatenate([kh] * HEADS_PER_VREG, axis=3)
    vt = v.transpose(0, 2, 3, 1)
    vt = jnp.concatenate([vt, jnp.ones((BATCH, N_KV, LANES - HEAD_DIM, LK), bf16)], axis=2)
    qt = qkv[:N_LAT, :N_HEADS * HEAD_DIM].T
    o = _attention(qt, kk, vt).T
    xl = _attn_out(o, x_all, mods[1], b_w_o[0].astype(bf16))
    xl = _moe(xl, mods[1], norm_g[1, 1][None, :], *moe_weights(0))

    xl = _pool_mixer(xl, mods[2], norm_g[2, 0][None, :], p_w[0].astype(bf16), p_scale[0][None, :])
    xl = _dense_ffn(xl, mods[2], norm_g[2, 1][None, :], *ffn_w, 1)

    xl = _gmlp_mixer(xl, x_ctx, False, mods[3], norm_g[3, 0][None, :], *gmlp_weights(1))
    xl = _moe(xl, mods[3], norm_g[3, 1][None, :], *moe_weights(1))
    return xl.reshape(BATCH, SEQ, D)
```

```python
import functools

import jax
import jax.numpy as jnp
import numpy as np
from jax import lax
from jax.experimental import pallas as pl
from jax.experimental.pallas import tpu as pltpu

D = 1024
BATCH = 4
SEQ = 4096
DEPTH = 4
GRID_W = 64
CTX_LEN = 256
CHUNK = 128
A_DIM = 2 * D
A_GROUPS = 8
A_GW = A_DIM // A_GROUPS
N_HEADS = 16
N_KV = 4
HEAD_DIM = 64
ROPE_FREQS = HEAD_DIM // 4
ROPE_THETA = 10000.0
POOL_WINDOWS = (2, 4, 8, 16)
POOL_GROUP = D // 4
FFN_DIM = 2816
N_EXPERTS = 8
EXPERT_DIM = 3584
EPS = 1e-6

N_LAT = BATCH * SEQ
N_CTX = BATCH * CTX_LEN
N_ALL = N_LAT + N_CTX
LK = CTX_LEN + SEQ
MOD_ROWS = 8
LANES = 128
QK_COLS = (N_HEADS + N_KV) * HEAD_DIM
QKV_COLS = (N_HEADS + 2 * N_KV) * HEAD_DIM

VMEM_LIMIT = 56 * 1024 * 1024

bf16 = jnp.bfloat16
f32 = jnp.float32


def _params(sem, vmem=VMEM_LIMIT):
    return pltpu.CompilerParams(dimension_semantics=sem, vmem_limit_bytes=vmem)


def _mod_row(t, tm):
    start = t * tm
    return jnp.where(start < N_LAT, start // SEQ, BATCH)


def _mod_spec(chunk, tm):
    return pl.BlockSpec((None, None, 1, D), lambda t, *_: (chunk, _mod_row(t, tm), 0, 0))


def _full_spec(shape):
    nd = len(shape)
    return pl.BlockSpec(shape, lambda *_: (0,) * nd)


def _modulate(x, g, shift, scale):
    ms = jnp.mean(x * x, axis=-1, keepdims=True)
    return (x * lax.rsqrt(ms + EPS) * g) * (1.0 + scale) + shift


LOG2E = 1.4426950408889634


def _gelu_tanh(z):
    k0 = -2.0 * 0.7978845608028654 * LOG2E
    return z / (1.0 + jnp.exp2(z * (z * z * (k0 * 0.044715) + k0)))


def _dot(a, b):
    return jnp.dot(a, b, preferred_element_type=f32)


ADA_TN = 1536


def _ada_kernel(cond_ref, w_ref, b_ref, o_ref):
    cond = cond_ref[...]
    s = (cond * jax.nn.sigmoid(cond)).astype(bf16)
    o_ref[...] = _dot(s, w_ref[...].astype(bf16)) + b_ref[...]


def _ada_all(cond, ada_w, ada_b):
    return pl.pallas_call(
        _ada_kernel,
        out_shape=jax.ShapeDtypeStruct((DEPTH, MOD_ROWS, 6 * D), f32),
        grid=(DEPTH, 6 * D // ADA_TN),
        in_specs=[
            _full_spec((MOD_ROWS, D)),
            pl.BlockSpec((None, D, ADA_TN), lambda i, n: (i, 0, n)),
            pl.BlockSpec((None, 1, ADA_TN), lambda i, n: (i, 0, n)),
        ],
        out_specs=pl.BlockSpec((None, MOD_ROWS, ADA_TN), lambda i, n: (i, 0, n)),
        compiler_params=_params(("parallel", "parallel")),
        name="ada",
    )(cond, ada_w, ada_b.reshape(DEPTH, 1, 6 * D))


GMLP_TM = 512


def _gmlp_kernel(x_ref, xc_ref, sh_ref, sc_ref, gt_ref, g_ref, win_ref, vg_ref, ws_ref, bs_ref, wout_ref,
                 o_ref, uv_ref):
    x = jnp.where(pl.program_id(0) < N_LAT // GMLP_TM, x_ref[...], xc_ref[...])
    h = _modulate(x, g_ref[...], sh_ref[...], sc_ref[...]).astype(bf16)
    z = _gelu_tanh(_dot(h, win_ref[...]))
    u = z[:, :A_DIM]
    v = z[:, A_DIM:]
    v = (v * lax.rsqrt(jnp.mean(v * v, axis=-1, keepdims=True) + EPS) * vg_ref[...]).astype(bf16)
    for c in range(GMLP_TM // CHUNK):
        r0 = c * CHUNK
        for g in range(A_GROUPS):
            c0 = g * A_GW
            mixed = _dot(ws_ref[g], v[r0:r0 + CHUNK, c0:c0 + A_GW]) + bs_ref[:, c0:c0 + A_GW]
            uv_ref[r0:r0 + CHUNK, c0:c0 + A_GW] = (u[r0:r0 + CHUNK, c0:c0 + A_GW] * mixed).astype(bf16)
    y = _dot(uv_ref[...], wout_ref[...])
    o_ref[...] = x + gt_ref[...] * y


def _gmlp_mixer(x, xc, with_ctx, mods, g, w_in, v_g, w_s, b_full, w_out, layer):
    rows = N_ALL if with_ctx else N_LAT
    tm = GMLP_TM
    lat_tiles = N_LAT // tm
    return pl.pallas_call(
        _gmlp_kernel,
        out_shape=jax.ShapeDtypeStruct((rows, D), f32),
        grid=(rows // tm,),
        in_specs=[
            pl.BlockSpec((tm, D), lambda t: (jnp.minimum(t, lat_tiles - 1), 0)),
            pl.BlockSpec((tm, D), lambda t: (jnp.maximum(t - lat_tiles, 0), 0)),
            _mod_spec(0, tm), _mod_spec(1, tm), _mod_spec(2, tm),
            _full_spec((1, D)),
            _resident_spec((D, 2 * A_DIM), layer),
            _full_spec((1, A_DIM)),
            _full_spec((A_GROUPS, CHUNK, CHUNK)),
            _full_spec((CHUNK, A_DIM)),
            _resident_spec((A_DIM, D), layer),
        ],
        out_specs=pl.BlockSpec((tm, D), lambda t: (t, 0)),
        scratch_shapes=[pltpu.VMEM((tm, A_DIM), bf16)],
        compiler_params=_params(("parallel",)),
        name="gmlp_mixer",
    )(x, xc, mods, mods, mods, g, w_in, v_g, w_s, b_full, w_out)


FFN_TM = 512


def _ffn_kernel(x_ref, sh_ref, sc_ref, gt_ref, g_ref, wg_ref, wu_ref, wd_ref, o_ref):
    x = x_ref[...]
    h = _modulate(x, g_ref[...], sh_ref[...], sc_ref[...]).astype(bf16)
    a = _dot(h, wg_ref[...])
    b = _dot(h, wu_ref[...])
    act = (a * jax.nn.sigmoid(a) * b).astype(bf16)
    o_ref[...] = x + gt_ref[...] * _dot(act, wd_ref[...])


def _resident_spec(shape, layer):
    nd = len(shape)
    return pl.BlockSpec((None,) + shape, lambda *_: (layer,) + (0,) * nd, pipeline_mode=pl.Buffered(1))


def _dense_ffn(x, mods, g, wg, wu, wd, layer):
    rows = x.shape[0]
    tm = FFN_TM
    return pl.pallas_call(
        _ffn_kernel,
        out_shape=jax.ShapeDtypeStruct((rows, D), f32),
        grid=(rows // tm,),
        in_specs=[
            pl.BlockSpec((tm, D), lambda t: (t, 0)),
            _mod_spec(3, tm), _mod_spec(4, tm), _mod_spec(5, tm),
            _full_spec((1, D)),
            _resident_spec((D, FFN_DIM), layer),
            _resident_spec((D, FFN_DIM), layer),
            _resident_spec((FFN_DIM, D), layer),
        ],
        out_specs=pl.BlockSpec((tm, D), lambda t: (t, 0)),
        compiler_params=_params(("parallel",)),
        name="dense_ffn",
    )(x, mods, mods, mods, g, wg, wu, wd)


QKV_TM = 512
HEADS_PER_VREG = LANES // HEAD_DIM
Q_SCALE = HEAD_DIM ** -0.5 * LOG2E


def _qkv_kernel(x_ref, sh_ref, sc_ref, g_ref, w_ref, gain_ref, bd_ref, cos_ref, sin_ref, o_ref):
    h = _modulate(x_ref[...], g_ref[...], sh_ref[...], sc_ref[...]).astype(bf16)
    qkv = _dot(h, w_ref[...])
    lane = lax.broadcasted_iota(jnp.int32, (QKV_TM, LANES), 1)
    first_half = (lane % (2 * ROPE_FREQS)) < ROPE_FREQS
    cos = cos_ref[...]
    sin = sin_ref[...]
    bd = bd_ref[...]
    for cb in range(QKV_COLS // LANES):
        c0 = cb * LANES
        blk = qkv[:, c0:c0 + LANES]
        if c0 < QK_COLS:
            sq = blk * blk
            hi = sq.astype(bf16)
            lo = (sq - hi.astype(f32)).astype(bf16)
            ms = (_dot(hi, bd) + _dot(lo, bd)) * (1.0 / HEAD_DIM)
            y = blk * lax.rsqrt(ms + EPS) * gain_ref[:, c0:c0 + LANES]
            partner = jnp.where(first_half, pltpu.roll(y, LANES - ROPE_FREQS, 1),
                                pltpu.roll(y, ROPE_FREQS, 1))
            y = y * cos + partner * sin
            if c0 < N_HEADS * HEAD_DIM:
                y = y * Q_SCALE
            blk = y
        o_ref[:, c0:c0 + LANES] = blk.astype(bf16)


def _qkv_proj(x_all, mods, g, w_qkv, gain, bd, cos_t, sin_t):
    tm = QKV_TM
    tiles_per_seq = SEQ // tm

    def tab_idx(t):
        return (jnp.where(t * tm < N_LAT, t % tiles_per_seq, tiles_per_seq), 0)

    return pl.pallas_call(
        _qkv_kernel,
        out_shape=jax.ShapeDtypeStruct((N_ALL, QKV_COLS), bf16),
        grid=(N_ALL // tm,),
        in_specs=[
            pl.BlockSpec((tm, D), lambda t: (t, 0)),
            _mod_spec(0, tm), _mod_spec(1, tm),
            _full_spec((1, D)),
            _full_spec((D, QKV_COLS)),
            _full_spec((1, QK_COLS)),
            _full_spec((LANES, LANES)),
            pl.BlockSpec((tm, LANES), tab_idx),
            pl.BlockSpec((tm, LANES), tab_idx),
        ],
        out_specs=pl.BlockSpec((tm, QKV_COLS), lambda t: (t, 0)),
        compiler_params=_params(("parallel",)),
        name="qkv_proj",
    )(x_all, mods, mods, g, w_qkv, gain, bd, cos_t, sin_t)


ATT_TQ = 512
REP = N_HEADS // N_KV
ATT_GROUPS = 2
ATT_HEADS = ATT_GROUPS * REP
ATT_KC = 256
ATT_LAG = 2


def _attn_kernel(qt_ref, kk_ref, vt_ref, o_ref, *st_refs):
    row = lax.broadcasted_iota(jnp.int32, (LANES, ATT_TQ), 0)
    low = row < HEAD_DIM
    zero = jnp.zeros((LANES, ATT_TQ), bf16)
    n_chunks = LK // ATT_KC
    sub = 8
    n_buf = len(st_refs)

    def masked_q(h):
        q2 = qt_ref[(h // HEADS_PER_VREG) * LANES:(h // HEADS_PER_VREG + 1) * LANES, :]
        return jnp.where(low if h % HEADS_PER_VREG == 0 else ~low, q2, zero)

    col_max = {}
    for ph in range(ATT_HEADS + ATT_LAG):
        ha, hb = ph, ph - ATT_LAG
        qm = masked_q(ha) if ha < ATT_HEADS else None
        mx = None
        acc = None
        for c in range(n_chunks):
            rows = slice(c * ATT_KC, (c + 1) * ATT_KC)
            if ha < ATT_HEADS:
                st = _dot(kk_ref[ha // REP, rows, :], qm)
                st_refs[ha % n_buf][rows, :] = st
                cm = jnp.max(st.reshape(ATT_KC // sub, sub, ATT_TQ), axis=0)
                mx = cm if mx is None else jnp.maximum(mx, cm)
            if hb >= 0:
                pt = jnp.exp2(st_refs[hb % n_buf][rows, :] - col_max[hb]).astype(bf16)
                part = _dot(vt_ref[hb // REP, :, rows], pt)
                acc = part if acc is None else acc + part
        if hb >= 0:
            o_ref[hb * HEAD_DIM:(hb + 1) * HEAD_DIM, :] = (
                acc[:HEAD_DIM] / acc[HEAD_DIM:HEAD_DIM + 1]).astype(bf16)
        if ha < ATT_HEADS:
            col_max[ha] = jnp.max(mx, axis=0, keepdims=True)


def _attention(qt, kk, vt):
    tq = ATT_TQ
    qt_per_seq = SEQ // tq
    gw = ATT_HEADS * HEAD_DIM
    return pl.pallas_call(
        _attn_kernel,
        out_shape=jax.ShapeDtypeStruct((D, N_LAT), bf16),
        grid=(BATCH, N_KV // ATT_GROUPS, qt_per_seq),
        in_specs=[
            pl.BlockSpec((gw, tq), lambda b, g, t: (g, b * qt_per_seq + t)),
            pl.BlockSpec((None, ATT_GROUPS, LK, LANES), lambda b, g, t: (b, g, 0, 0)),
            pl.BlockSpec((None, ATT_GROUPS, LANES, LK), lambda b, g, t: (b, g, 0, 0)),
        ],
        out_specs=pl.BlockSpec((gw, tq), lambda b, g, t: (g, b * qt_per_seq + t)),
        scratch_shapes=[pltpu.VMEM((LK, tq), f32)] * (ATT_LAG + 1),
        compiler_params=_params(("parallel", "parallel", "arbitrary")),
        name="attention",
    )(qt, kk, vt)


PROJ_TM = 512


def _oproj_kernel(o_ref, x_ref, gt_ref, w_ref, out_ref):
    out_ref[...] = x_ref[...] + gt_ref[...] * _dot(o_ref[...], w_ref[...])


def _attn_out(o, x_all, mods, w_o):
    tm = PROJ_TM
    return pl.pallas_call(
        _oproj_kernel,
        out_shape=jax.ShapeDtypeStruct((N_LAT, D), f32),
        grid=(N_LAT // tm,),
        in_specs=[
            pl.BlockSpec((tm, D), lambda t: (t, 0)),
            pl.BlockSpec((tm, D), lambda t: (t, 0)),
            _mod_spec(2, tm),
            _full_spec((D, D)),
        ],
        out_specs=pl.BlockSpec((tm, D), lambda t: (t, 0)),
        compiler_params=_params(("parallel",)),
        name="attn_out",
    )(o, x_all, mods, w_o)


POOL_TM = 512
POOL_HALO = 8
POOL_EXT = POOL_TM + 2 * POOL_HALO
assert max(POOL_WINDOWS) // 2 <= POOL_HALO


def _pool_kernel(xp_ref, x_ref, xn_ref, sh_ref, sc_ref, gt_ref, g_ref, pw_ref, ps_ref, o_ref):
    t = pl.program_id(0)
    tiles_per_seq = SEQ // POOL_TM
    ts = t % tiles_per_seq
    x = x_ref[...]
    g, sh, sc = g_ref[...], sh_ref[...], sc_ref[...]
    h = _modulate(x, g, sh, sc)
    hp = jnp.where(ts > 0, _modulate(xp_ref[...], g, sh, sc), 0.0)
    hn = jnp.where(ts < tiles_per_seq - 1, _modulate(xn_ref[...], g, sh, sc), 0.0)
    ext = jnp.concatenate([hp, h, hn], axis=0)
    pos = ts * POOL_TM + lax.broadcasted_iota(jnp.int32, (POOL_TM, 1), 0)
    for j, w in enumerate(POOL_WINDOWS):
        c0 = j * POOL_GROUP
        p = ext[:, c0:c0 + POOL_GROUP]
        k = 1
        while k < w:
            p = p + pltpu.roll(p, POOL_EXT - k, 0)
            k *= 2
        first = POOL_HALO - w // 2
        s = (pltpu.roll(p, POOL_EXT - first, 0) if first else p)[:POOL_TM]
        lo_i = jnp.maximum(pos - w // 2, 0)
        hi_i = jnp.minimum(pos + w // 2 - 1, SEQ - 1)
        cnt = (hi_i - lo_i + 1).astype(f32)
        d = (s / cnt - h[:, c0:c0 + POOL_GROUP]).astype(bf16)
        y = _dot(d, pw_ref[j]) * ps_ref[:, c0:c0 + POOL_GROUP]
        o_ref[:, c0:c0 + POOL_GROUP] = x[:, c0:c0 + POOL_GROUP] + gt_ref[:, c0:c0 + POOL_GROUP] * y


def _pool_mixer(x, mods, g, p_w, p_scale):
    tm = POOL_TM
    hb = tm // POOL_HALO
    n_halo_blocks = N_LAT // POOL_HALO
    return pl.pallas_call(
        _pool_kernel,
        out_shape=jax.ShapeDtypeStruct((N_LAT, D), f32),
        grid=(N_LAT // tm,),
        in_specs=[
            pl.BlockSpec((POOL_HALO, D), lambda t: (jnp.maximum(t * hb - 1, 0), 0)),
            pl.BlockSpec((tm, D), lambda t: (t, 0)),
            pl.BlockSpec((POOL_HALO, D), lambda t: (jnp.minimum((t + 1) * hb, n_halo_blocks - 1), 0)),
            _mod_spec(0, tm), _mod_spec(1, tm), _mod_spec(2, tm),
            _full_spec((1, D)),
            _full_spec((len(POOL_WINDOWS), POOL_GROUP, POOL_GROUP)),
            _full_spec((1, D)),
        ],
        out_specs=pl.BlockSpec((tm, D), lambda t: (t, 0)),
        compiler_params=_params(("parallel",)),
        name="pool_mixer",
    )(x, x, x, mods, mods, mods, g, p_w, p_scale)


ROUTER_TM = 512


def _router_kernel(x_ref, sh_ref, sc_ref, g_ref, wr_ref, ltri_ref, h_ref, route_ref, cnt_ref, run_ref):
    @pl.when(pl.program_id(0) == 0)
    def _():
        run_ref[...] = jnp.zeros_like(run_ref)

    h = _modulate(x_ref[...], g_ref[...], sh_ref[...], sc_ref[...])
    h_ref[...] = h
    h_hi = h.astype(bf16)
    h_lo = (h - h_hi.astype(f32)).astype(bf16)
    p1 = _dot(h_hi, wr_ref[0])
    logits = p1 + pltpu.roll(p1, LANES - N_EXPERTS, 1) + _dot(h_lo, wr_ref[1])
    lane = lax.broadcasted_iota(jnp.int32, logits.shape, 1)
    neg = jnp.float32(-jnp.inf)
    logits = jnp.where(lane < N_EXPERTS, logits, neg)
    m1 = jnp.max(logits, axis=-1, keepdims=True)
    i1 = jnp.min(jnp.where(logits == m1, lane, LANES), axis=-1, keepdims=True)
    rest = jnp.where(lane == i1, neg, logits)
    m2 = jnp.max(rest, axis=-1, keepdims=True)
    i2 = jnp.min(jnp.where(rest == m2, lane, LANES), axis=-1, keepdims=True)
    e2 = jnp.exp(m2 - m1)
    g1 = 1.0 / (1.0 + e2)
    g2 = e2 / (1.0 + e2)
    oh1 = jnp.where(lane == i1, 1.0, 0.0)
    oh2 = jnp.where(lane == i2, 1.0, 0.0)
    ltri = ltri_ref[...]
    before1 = _dot(ltri, oh1.astype(bf16))
    before2 = _dot(ltri, oh2.astype(bf16))
    tot1 = jnp.sum(oh1, axis=0, keepdims=True)
    tot2 = jnp.sum(oh2, axis=0, keepdims=True)
    run = run_ref[...]
    lp1 = jnp.sum(oh1 * (run + before1), axis=-1, keepdims=True)
    lp2 = jnp.sum(oh2 * (run + tot1 + before2), axis=-1, keepdims=True)
    run = run + tot1 + tot2
    run_ref[...] = run
    cnt_ref[...] = run
    route = jnp.where(lane == 0, i1.astype(f32),
                      jnp.where(lane == 1, i2.astype(f32),
                                jnp.where(lane == 2, g1,
                                          jnp.where(lane == 3, g2,
                                                    jnp.where(lane == 4, lp1, jnp.where(lane == 5, lp2, 0.0))))))
    route_ref[...] = route


def _router(x, mods, g, wr_pad):
    tm = ROUTER_TM
    ltri = jnp.asarray(np.tril(np.ones((tm, tm), np.float32), -1), dtype=bf16)
    return pl.pallas_call(
        _router_kernel,
        out_shape=(jax.ShapeDtypeStruct((N_LAT, D), f32),
                   jax.ShapeDtypeStruct((N_LAT, LANES), f32),
                   jax.ShapeDtypeStruct((1, LANES), f32)),
        grid=(N_LAT // tm,),
        in_specs=[
            pl.BlockSpec((tm, D), lambda t: (t, 0)),
            _mod_spec(3, tm), _mod_spec(4, tm),
            _full_spec((1, D)),
            _full_spec((2, D, LANES)),
            _full_spec((tm, tm)),
        ],
        out_specs=(pl.BlockSpec((tm, D), lambda t: (t, 0)),
                   pl.BlockSpec((tm, LANES), lambda t: (t, 0)),
                   _full_spec((1, LANES))),
        scratch_shapes=[pltpu.VMEM((1, LANES), f32)],
        compiler_params=_params(("arbitrary",)),
        name="router",
    )(x, mods, mods, g, wr_pad, ltri)


CAST_BLOCK_BYTES = 8 * 1024 * 1024


def _cast_kernel(w_ref, o_ref):
    o_ref[...] = w_ref[...].astype(bf16)


def _layer_weights_bf16(w, layer):
    n_layers, n_e, r, c = w.shape
    rows = n_e * r
    block_rows = 1 << ((CAST_BLOCK_BYTES // (4 * c)).bit_length() - 1)
    assert rows % block_rows == 0
    steps = rows // block_rows
    out = pl.pallas_call(
        _cast_kernel,
        out_shape=jax.ShapeDtypeStruct((rows, c), bf16),
        grid=(steps,),
        in_specs=[pl.BlockSpec((block_rows, c), lambda i: (layer * steps + i, 0))],
        out_specs=pl.BlockSpec((block_rows, c), lambda i: (i, 0)),
        compiler_params=_params(("parallel",)),
        name="cast_expert_weights",
    )(w.reshape(n_layers * rows, c))
    return out.reshape(n_e, r, c)


MOE_TM = 512
MOE_TF = 1792
MOE_TILES = (2 * N_LAT) // MOE_TM + N_EXPERTS
MOE_ROWS = MOE_TILES * MOE_TM
MOE_XS_PARTS = 4
MOE_PART_TILES = MOE_TILES // MOE_XS_PARTS
assert MOE_PART_TILES * MOE_XS_PARTS == MOE_TILES


def _expert_kernel(te_ref, nt_ref, *refs):
    xs_refs = refs[:MOE_XS_PARTS]
    wg_ref, wu_ref, wd_ref, y_ref, xb_ref, acc_ref = refs[MOE_XS_PARTS:]
    t = pl.program_id(0)
    j = pl.program_id(1)
    last = pl.num_programs(1) - 1
    active = t < nt_ref[0]

    @pl.when(active)
    def _():
        @pl.when(j == 0)
        def _():
            for k, xs_ref in enumerate(xs_refs):
                @pl.when(t // MOE_PART_TILES == k)
                def _():
                    xb_ref[...] = xs_ref[...].astype(bf16)
            acc_ref[...] = jnp.zeros_like(acc_ref)

        xs = xb_ref[...]
        a = _dot(xs, wg_ref[...])
        b = _dot(xs, wu_ref[...])
        act = (a * jax.nn.sigmoid(a) * b).astype(bf16)
        acc_ref[...] += _dot(act, wd_ref[...])

        @pl.when(j == last)
        def _():
            y_ref[...] = acc_ref[...]

    @pl.when(jnp.logical_not(active) & (j == last))
    def _():
        y_ref[...] = jnp.zeros_like(y_ref)


def _experts(tile_expert, num_tiles, xs_parts, wg, wu, wd):
    tm, tf = MOE_TM, MOE_TF
    nj = EXPERT_DIM // tf

    def xs_spec(k):
        def index(t, j, te, nt):
            tt = jnp.minimum(t, jnp.maximum(nt[0] - 1, 0))
            return (jnp.clip(tt - k * MOE_PART_TILES, 0, MOE_PART_TILES - 1), 0)
        return pl.BlockSpec((tm, D), index)

    def w_col(t, j, te, nt):
        jj = jnp.where(t < nt[0], j, nj - 1)
        return (te[t], 0, jj)

    def w_row(t, j, te, nt):
        jj = jnp.where(t < nt[0], j, nj - 1)
        return (te[t], jj, 0)

    return pl.pallas_call(
        _expert_kernel,
        out_shape=jax.ShapeDtypeStruct((MOE_ROWS, D), f32),
        grid_spec=pltpu.PrefetchScalarGridSpec(
            num_scalar_prefetch=2,
            grid=(MOE_TILES, nj),
            in_specs=[xs_spec(k) for k in range(MOE_XS_PARTS)] + [
                pl.BlockSpec((None, D, tf), w_col),
                pl.BlockSpec((None, D, tf), w_col),
                pl.BlockSpec((None, tf, D), w_row),
            ],
            out_specs=pl.BlockSpec((tm, D), lambda t, j, te, nt: (t, 0)),
            scratch_shapes=[pltpu.VMEM((tm, D), bf16), pltpu.VMEM((tm, D), f32)],
        ),
        compiler_params=_params(("arbitrary", "arbitrary")),
        name="moe_experts",
    )(tile_expert, num_tiles, *xs_parts, wg, wu, wd)


COMBINE_TM = 512


def _combine_kernel(x_ref, y1_ref, y2_ref, route_ref, gt_ref, o_ref):
    route = route_ref[...]
    mix = route[:, 2:3] * y1_ref[...] + route[:, 3:4] * y2_ref[...]
    o_ref[...] = x_ref[...] + gt_ref[...] * mix


def _combine(x, y1, y2, route, mods):
    tm = COMBINE_TM
    return pl.pallas_call(
        _combine_kernel,
        out_shape=jax.ShapeDtypeStruct((N_LAT, D), f32),
        grid=(N_LAT // tm,),
        in_specs=[
            pl.BlockSpec((tm, D), lambda t: (t, 0)),
            pl.BlockSpec((tm, D), lambda t: (t, 0)),
            pl.BlockSpec((tm, D), lambda t: (t, 0)),
            pl.BlockSpec((tm, LANES), lambda t: (t, 0)),
            _mod_spec(5, tm),
        ],
        out_specs=pl.BlockSpec((tm, D), lambda t: (t, 0)),
        compiler_params=_params(("parallel",)),
        name="moe_combine",
    )(x, y1, y2, route, mods)


def _expert_table(table, e):
    ids = jnp.arange(N_EXPERTS, dtype=jnp.int32)
    return jnp.sum(jnp.where(e[:, None] == ids[None, :], table[None, :], 0), axis=1)


def _take_rows(a, idx):
    return a.at[idx].get(mode="promise_in_bounds")


def _moe(x, mods, g, wr_pad, wg, wu, wd, expert_base):
    tm = MOE_TM
    hp, route, cnt = _router(x, mods, g, wr_pad)
    e1 = route[:, 0].astype(jnp.int32)
    e2 = route[:, 1].astype(jnp.int32)
    counts = cnt[0, :N_EXPERTS].astype(jnp.int32)
    tiles_per_e = (counts + tm - 1) // tm
    tile_end = jnp.cumsum(tiles_per_e)
    group_base = (tile_end - tiles_per_e) * tm
    dense_base = jnp.cumsum(counts) - counts
    pos1 = _expert_table(group_base, e1) + route[:, 4].astype(jnp.int32)
    pos2 = _expert_table(group_base, e2) + route[:, 5].astype(jnp.int32)
    num_tiles = tile_end[-1]
    tile_ids = jnp.arange(MOE_TILES, dtype=jnp.int32)
    tile_expert = jnp.sum(tile_end[None, :] <= jnp.minimum(tile_ids, num_tiles - 1)[:, None], axis=1)
    tile_expert = jnp.minimum(tile_expert, N_EXPERTS - 1).astype(jnp.int32)
    tok = jnp.arange(N_LAT, dtype=jnp.int32)
    _, sorted_tok = lax.sort_key_val(jnp.concatenate([pos1, pos2]), jnp.concatenate([tok, tok]))
    row_expert = jnp.repeat(tile_expert, tm)
    src = jnp.arange(MOE_ROWS, dtype=jnp.int32) - _expert_table(group_base - dense_base, row_expert)
    row_token = _take_rows(sorted_tok, jnp.clip(src, 0, 2 * N_LAT - 1))
    part_rows = MOE_PART_TILES * tm
    xs_parts = [_take_rows(hp, row_token[k * part_rows:(k + 1) * part_rows]) for k in range(MOE_XS_PARTS)]
    y = _experts(tile_expert + expert_base, num_tiles.reshape(1).astype(jnp.int32), xs_parts, wg, wu, wd)
    return _combine(x, _take_rows(y, pos1), _take_rows(y, pos2), route, mods)


def _rope_tables():
    rows = SEQ // GRID_W
    r = jnp.repeat(jnp.arange(rows), GRID_W)
    col = jnp.tile(jnp.arange(GRID_W), rows)
    inv = ROPE_THETA ** (-jnp.arange(ROPE_FREQS, dtype=f32) / ROPE_FREQS)
    ang = jnp.stack([r, col], axis=-1).astype(f32)[..., None] * inv
    c = jnp.cos(ang)
    s = jnp.sin(ang)
    cos_h = jnp.concatenate([c[:, 0], c[:, 0], c[:, 1], c[:, 1]], axis=-1)
    sin_h = jnp.concatenate([-s[:, 0], s[:, 0], -s[:, 1], s[:, 1]], axis=-1)
    cos_t = jnp.tile(cos_h, (1, HEADS_PER_VREG))
    sin_t = jnp.tile(sin_h, (1, HEADS_PER_VREG))
    cos_t = jnp.concatenate([cos_t, jnp.ones((QKV_TM, LANES), f32)], axis=0)
    sin_t = jnp.concatenate([sin_t, jnp.zeros((QKV_TM, LANES), f32)], axis=0)
    return cos_t, sin_t


def kernel(x, c, ctx, c_ctx, ada_w, ada_b, norm_g, a_w_in, a_v_g, a_ws, a_bs, a_w_out, b_w_qkv, b_q_g, b_k_g,
           b_w_o, p_w, p_scale, f_w_gate, f_w_up, f_w_down, m_router, m_w_gate, m_w_up, m_w_down):
    cond = jnp.concatenate([c, c_ctx[None, :], jnp.zeros((MOD_ROWS - BATCH - 1, D), f32)], axis=0)
    ada = _ada_all(cond, ada_w, ada_b)
    mods = ada.reshape(DEPTH, MOD_ROWS, 6, 1, D).transpose(0, 2, 1, 3, 4)

    gmlp_w_in = a_w_in.astype(bf16)
    gmlp_w_out = a_w_out.astype(bf16)

    def gmlp_weights(j):
        b_full = jnp.repeat(a_bs[j].T, A_GW, axis=1)
        return gmlp_w_in, a_v_g[j][None, :], a_ws[j].astype(bf16), b_full, gmlp_w_out, j

    def moe_weights(f):
        w_hi = m_router[f].astype(bf16)
        w_lo = (m_router[f] - w_hi.astype(f32)).astype(bf16)
        zeros = jnp.zeros((D, LANES - 2 * N_EXPERTS), bf16)
        wr = jnp.stack([jnp.concatenate([w_hi, w_lo, zeros], axis=1),
                        jnp.concatenate([w_hi, jnp.zeros_like(w_lo), zeros], axis=1)])
        return (wr, _layer_weights_bf16(m_w_gate, f), _layer_weights_bf16(m_w_up, f),
                _layer_weights_bf16(m_w_down, f), 0)

    x_lat = x.reshape(N_LAT, D)
    x_ctx = ctx.reshape(N_CTX, D)

    x_all = _gmlp_mixer(x_lat, x_ctx, True, mods[0], norm_g[0, 0][None, :], *gmlp_weights(0))
    ffn_w = (f_w_gate.astype(bf16), f_w_up.astype(bf16), f_w_down.astype(bf16))
    x_all = _dense_ffn(x_all, mods[0], norm_g[0, 1][None, :], *ffn_w, 0)

    gain = jnp.concatenate([jnp.tile(b_q_g[0], N_HEADS), jnp.tile(b_k_g[0], N_KV)])[None, :]
    head_of_lane = np.arange(LANES) // HEAD_DIM
    bd = jnp.asarray((head_of_lane[:, None] == head_of_lane[None, :]).astype(np.float32), dtype=bf16)
    cos_t, sin_t = _rope_tables()
    qkv = _qkv_proj(x_all, mods[1], norm_g[1, 0][None, :], b_w_qkv[0].astype(bf16), gain, bd, cos_t, sin_t)

    def keys_first(a):
        return jnp.concatenate([a[N_LAT:].reshape(BATCH, CTX_LEN, -1), a[:N_LAT].reshape(BATCH, SEQ, -1)], axis=1)

    k = keys_first(qkv[:, N_HEADS * HEAD_DIM:QK_COLS]).reshape(BATCH, LK, N_KV, HEAD_DIM)
    v = keys_first(qkv[:, QK_COLS:]).reshape(BATCH, LK, N_KV, HEAD_DIM)
    kh = k.transpose(0, 2, 1, 3)
    kk = jnp.concatenate([kh] * HEADS_PER_VREG, axis=3)
    vt = v.transpose(0, 2, 3, 1)
    vt = jnp.concatenate([vt, jnp.ones((BATCH, N_KV, LANES - HEAD_DIM, LK), bf16)], axis=2)
    qt = qkv[:N_LAT, :N_HEADS * HEAD_DIM].T
    o = _attention(qt, kk, vt).T
    xl = _attn_out(o, x_all, mods[1], b_w_o[0].astype(bf16))
    xl = _moe(xl, mods[1], norm_g[1, 1][None, :], *moe_weights(0))

    xl = _pool_mixer(xl, mods[2], norm_g[2, 0][None, :], p_w[0].astype(bf16), p_scale[0][None, :])
    xl = _dense_ffn(xl, mods[2], norm_g[2, 1][None, :], *ffn_w, 1)

    xl = _gmlp_mixer(xl, x_ctx, False, mods[3], norm_g[3, 0][None, :], *gmlp_weights(1))
    xl = _moe(xl, mods[3], norm_g[3, 1][None, :], *moe_weights(1))
    return xl.reshape(BATCH, SEQ, D)
```
